```python
import math
import jax, jax.numpy as jnp
from jax import lax
import numpy as np

D_MODEL = 1024
BATCH = 8
SEQ = 4096
DEPTH = 4

GRID_W = 64
CTX_LEN = 256
Q_BLOCK = 128
ROPE_BASE = 10000.0
EPS = 1e-6
ADA_SCALE = 0.02

DA_HEADS = 4
DA_HD = 64
DA_Q = DA_HEADS * 2 * DA_HD
DA_K = DA_HEADS * 2 * DA_HD
DA_V = DA_HEADS * 2 * DA_HD
MLA_HEADS = 4
MLA_NOPE = 128
MLA_ROPE = 64
MLA_V = 128
MLA_Q_RANK = 384
MLA_KV_RANK = 256
MLA_OUT = MLA_HEADS * MLA_V
W_IN_COLS = DA_Q + DA_K + DA_V + MLA_Q_RANK + MLA_KV_RANK + MLA_ROPE + 2 * D_MODEL
N_EXPERTS = 32
TOP_K = 4
D_FF = 512
SWIGLU_LIMIT = 7.0
SWIGLU_ALPHA = 1.702
EXPERT_BLOCK = 512

kernel_name = "hybrid_diffattn_mla_moe_prefix_dit"


def rms(x, g):
    xf = x.astype(jnp.float32)
    y = xf * lax.rsqrt(jnp.mean(xf * xf, axis=-1, keepdims=True) + EPS)
    return (y * g.astype(jnp.float32)).astype(x.dtype)


def modulate(x, g, shift, scale):
    return rms(x, g) * (1 + scale) + shift


def rotate_half(x):
    x1, x2 = jnp.split(x, 2, axis=-1)
    return jnp.concatenate([-x2, x1], axis=-1)


def rotate_axial(x):
    xr, xc = jnp.split(x, 2, axis=-1)
    return jnp.concatenate([rotate_half(xr), rotate_half(xc)], axis=-1)


def axial_tables(rows, cols, d_rot, dtype):
    half = d_rot // 2
    freqs = ROPE_BASE ** (-jnp.arange(0, half, 2, dtype=jnp.float32) / half)
    ar = rows[:, None] * freqs
    ac = cols[:, None] * freqs
    ang = jnp.concatenate([ar, ar, ac, ac], axis=-1)
    return jnp.cos(ang)[:, None, :].astype(dtype), jnp.sin(ang)[:, None, :].astype(dtype)


def apply_rope(x, tables):
    cos, sin = tables
    return x * cos + rotate_axial(x) * sin


def softmax32(s):
    return jax.nn.softmax(s.astype(jnp.float32), axis=-1)


def diff_core(q1, q2, k1, k2, v, lam):
    scale = 1.0 / math.sqrt(DA_HD)
    p1 = softmax32(jnp.einsum('bqhd,bkhd->bhqk', q1, k1) * scale)
    p2 = softmax32(jnp.einsum('bqhd,bkhd->bhqk', q2, k2) * scale)
    p = (p1 - lam * p2).astype(v.dtype)
    return jnp.einsum('bhqk,bkhe->bqhe', p, v)


def mla_core(qn, qr, kn, kr, v):
    scale = 1.0 / math.sqrt(MLA_NOPE + MLA_ROPE)
    s = jnp.einsum('bqhd,bkhd->bhqk', qn, kn) + jnp.einsum('bqhd,bkd->bhqk', qr, kr)
    p = softmax32(s * scale).astype(v.dtype)
    return jnp.einsum('bhqk,bkhe->bqhe', p, v)


def blocked(fn, *qs):
    B, S = qs[0].shape[:2]
    nb = S // Q_BLOCK
    split = lambda a: jnp.moveaxis(a.reshape(B, nb, Q_BLOCK, *a.shape[2:]), 1, 0)
    out = lax.map(lambda qb: fn(*qb), tuple(split(a) for a in qs))
    return jnp.moveaxis(out, 0, 1).reshape(B, S, *out.shape[3:])


def mixer_inputs(h, w_in, qa_g, kva_g, wqb, wkvb, rope_da, rope_mla):
    B, S, _ = h.shape
    z = h @ w_in
    o = np.cumsum([DA_Q, DA_K, DA_V, MLA_Q_RANK, MLA_KV_RANK, MLA_ROPE])
    da_q, da_k, da_v, cq, ckv, kr, gates = jnp.split(z, list(o), axis=-1)
    da_q = da_q.reshape(B, S, DA_HEADS, 2, DA_HD)
    da_k = da_k.reshape(B, S, DA_HEADS, 2, DA_HD)
    q1, q2 = da_q[..., 0, :], da_q[..., 1, :]
    k1, k2 = da_k[..., 0, :], da_k[..., 1, :]
    v = da_v.reshape(B, S, DA_HEADS, 2 * DA_HD)
    q_mla = (rms(cq, qa_g) @ wqb).reshape(B, S, MLA_HEADS, MLA_NOPE + MLA_ROPE)
    kv_mla = (rms(ckv, kva_g) @ wkvb).reshape(B, S, MLA_HEADS, MLA_NOPE + MLA_V)
    qn, qr = q_mla[..., :MLA_NOPE], q_mla[..., MLA_NOPE:]
    kn, vm = kv_mla[..., :MLA_NOPE], kv_mla[..., MLA_NOPE:]
    if rope_da is not None:
        q1, q2 = apply_rope(q1, rope_da), apply_rope(q2, rope_da)
        k1, k2 = apply_rope(k1, rope_da), apply_rope(k2, rope_da)
        qr = apply_rope(qr, rope_mla)
        kr = apply_rope(kr[:, :, None, :], rope_mla)[:, :, 0, :]
    return q1, q2, k1, k2, v, qn, qr, kn, kr, vm, gates


def merge(o_da, o_mla, gates, subln_g, lam_init, w_br_a, w_br_b, w_out):
    B, S = o_da.shape[:2]
    o_da = (rms(o_da, subln_g) * (1.0 - lam_init)).reshape(B, S, DA_V)
    o_mla = o_mla.reshape(B, S, MLA_OUT)
    ga, gb = jnp.split(jax.nn.sigmoid(gates), 2, axis=-1)
    return (ga * (o_da @ w_br_a) + gb * (o_mla @ w_br_b)) @ w_out


def moe(t, rw, rb, wgu, bgu, wdn, bdn):
    T, D = t.shape
    TK = T * TOP_K
    NB = (TK + EXPERT_BLOCK - 1) // EXPERT_BLOCK + N_EXPERTS
    logits = (t @ rw + rb).astype(jnp.float32)
    topv, topi = lax.top_k(logits, TOP_K)
    wts = jax.nn.softmax(topv, axis=-1).reshape(-1)
    flat_e = topi.reshape(-1)
    order = jnp.argsort(flat_e)
    e_sorted = flat_e[order]
    tok = order // TOP_K
    sizes = jnp.bincount(flat_e, length=N_EXPERTS).astype(jnp.int32)
    padded = ((sizes + EXPERT_BLOCK - 1) // EXPERT_BLOCK) * EXPERT_BLOCK
    start = jnp.cumsum(sizes) - sizes
    pend = jnp.cumsum(padded)
    pstart = pend - padded
    dest = pstart[e_sorted] + (jnp.arange(TK, dtype=jnp.int32) - start[e_sorted])
    buf = jnp.zeros((NB * EXPERT_BLOCK, D), t.dtype).at[dest].set(t[tok])
    block_e = jnp.minimum(jnp.searchsorted(pend, jnp.arange(NB, dtype=jnp.int32) * EXPERT_BLOCK,
                                           side='right'), N_EXPERTS - 1)
    xb = buf.reshape(NB, EXPERT_BLOCK, D)
    h = jnp.einsum('nbd,ndf->nbf', xb, wgu[block_e]) + bgu[block_e][:, None, :]
    gate, lin = jnp.split(h, 2, axis=-1)
    gate = jnp.minimum(gate, SWIGLU_LIMIT)
    lin = jnp.clip(lin, -SWIGLU_LIMIT, SWIGLU_LIMIT)
    act = (lin + 1) * gate * jax.nn.sigmoid(SWIGLU_ALPHA * gate)
    out = jnp.einsum('nbf,nfd->nbd', act, wdn[block_e]) + bdn[block_e][:, None, :]
    y = out.reshape(NB * EXPERT_BLOCK, D)[dest] * wts[order][:, None].astype(out.dtype)
    return jnp.zeros_like(t).at[tok].add(y)


def setup_inputs(seed: int = 0) -> dict:
    key = jax.random.key(seed)
    ks = jax.random.split(key, 32)
    f32 = jnp.float32
    n = lambda k, shape, s: jax.random.normal(k, shape, f32) * s
    L, D = DEPTH, D_MODEL
    return {
        "x": n(ks[0], (BATCH, SEQ, D), 1.0),
        "c": n(ks[1], (BATCH, D), 1.0),
        "ctx": n(ks[2], (BATCH, CTX_LEN, D), 1.0),
        "c_ctx": n(ks[3], (D,), 1.0),
        "ada_w": n(ks[4], (L, D, 6 * D), ADA_SCALE),
        "ada_b": n(ks[5], (L, 6 * D), ADA_SCALE),
        "norm1_g": 1.0 + n(ks[6], (L, D), 0.02),
        "norm2_g": 1.0 + n(ks[7], (L, D), 0.02),
        "w_in": n(ks[8], (L, D, W_IN_COLS), D ** -0.5),
        "da_lq1": n(ks[9], (L, DA_HD), 0.1),
        "da_lk1": n(ks[10], (L, DA_HD), 0.1),
        "da_lq2": n(ks[11], (L, DA_HD), 0.1),
        "da_lk2": n(ks[12], (L, DA_HD), 0.1),
        "da_subln_g": 1.0 + n(ks[13], (L, 2 * DA_HD), 0.02),
        "mla_qa_g": 1.0 + n(ks[14], (L, MLA_Q_RANK), 0.02),
        "mla_kva_g": 1.0 + n(ks[15], (L, MLA_KV_RANK), 0.02),
        "mla_wqb": n(ks[16], (L, MLA_Q_RANK, MLA_HEADS * (MLA_NOPE + MLA_ROPE)), MLA_Q_RANK ** -0.5),
        "mla_wkvb": n(ks[17], (L, MLA_KV_RANK, MLA_HEADS * (MLA_NOPE + MLA_V)), MLA_KV_RANK ** -0.5),
        "w_br_a": n(ks[18], (L, DA_V, D), DA_V ** -0.5),
        "w_br_b": n(ks[19], (L, MLA_OUT, D), MLA_OUT ** -0.5),
        "w_out": n(ks[20], (L, D, D), D ** -0.5),
        "router_w": n(ks[21], (L, D, N_EXPERTS), D ** -0.5),
        "router_b": n(ks[22], (L, N_EXPERTS), 0.01),
        "moe_wgu": n(ks[23], (L, N_EXPERTS, D, 2 * D_FF), D ** -0.5),
        "moe_bgu": n(ks[24], (L, N_EXPERTS, 2 * D_FF), 0.01),
        "moe_wdn": n(ks[25], (L, N_EXPERTS, D_FF, D), D_FF ** -0.5),
        "moe_bdn": n(ks[26], (L, N_EXPERTS, D), 0.01),
        "final_g": 1.0 + n(ks[27], (D,), 0.02),
    }


def reference(x, c, ctx, c_ctx, ada_w, ada_b, norm1_g, norm2_g, w_in, da_lq1, da_lk1, da_lq2, da_lk2,
              da_subln_g, mla_qa_g, mla_kva_g, mla_wqb, mla_wkvb, w_br_a, w_br_b, w_out,
              router_w, router_b, moe_wgu, moe_bgu, moe_wdn, moe_bdn, final_g):
    B, S, D = x.shape
    ROWS = S // GRID_W
    rows = jnp.repeat(jnp.arange(ROWS, dtype=jnp.float32), GRID_W)
    cols = jnp.tile(jnp.arange(GRID_W, dtype=jnp.float32), ROWS)
    rope_da = axial_tables(rows, cols, DA_HD, x.dtype)
    rope_mla = axial_tables(rows, cols, MLA_ROPE, x.dtype)
    y = ctx
    for l in range(DEPTH):
        last = l == DEPTH - 1
        mod_x = (jax.nn.silu(c) @ ada_w[l] + ada_b[l])[:, None, :]
        mod_y = (jax.nn.silu(c_ctx) @ ada_w[l] + ada_b[l])[None, None, :]
        sh1x, sc1x, g1x, sh2x, sc2x, g2x = jnp.split(mod_x, 6, axis=-1)
        sh1y, sc1y, g1y, sh2y, sc2y, g2y = jnp.split(mod_y, 6, axis=-1)

        hx = modulate(x, norm1_g[l], sh1x, sc1x)
        hy = modulate(y, norm1_g[l], sh1y, sc1y)
        proj = (w_in[l], mla_qa_g[l], mla_kva_g[l], mla_wqb[l], mla_wkvb[l])
        q1x, q2x, k1x, k2x, vx, qnx, qrx, knx, krx, vmx, gx = mixer_inputs(hx, *proj, rope_da, rope_mla)
        q1y, q2y, k1y, k2y, vy, qny, qry, kny, kry, vmy, gy = mixer_inputs(hy, *proj, None, None)

        lam_init = 0.8 - 0.6 * math.exp(-0.3 * l)
        lam = (jnp.exp(jnp.sum(da_lq1[l].astype(jnp.float32) * da_lk1[l].astype(jnp.float32)))
               - jnp.exp(jnp.sum(da_lq2[l].astype(jnp.float32) * da_lk2[l].astype(jnp.float32)))
               + lam_init)

        k1a = jnp.concatenate([k1x, k1y], axis=1)
        k2a = jnp.concatenate([k2x, k2y], axis=1)
        va = jnp.concatenate([vx, vy], axis=1)
        kna = jnp.concatenate([knx, kny], axis=1)
        kra = jnp.concatenate([krx, kry], axis=1)
        vma = jnp.concatenate([vmx, vmy], axis=1)
        o_da_x = blocked(lambda a, b: diff_core(a, b, k1a, k2a, va, lam), q1x, q2x)
        o_mla_x = blocked(lambda a, b: mla_core(a, b, kna, kra, vma), qnx, qrx)
        mix_x = merge(o_da_x, o_mla_x, gx, da_subln_g[l], lam_init, w_br_a[l], w_br_b[l], w_out[l])
        x = x + g1x * mix_x

        moe_p = (router_w[l], router_b[l], moe_wgu[l], moe_bgu[l], moe_wdn[l], moe_bdn[l])
        h2x = modulate(x, norm2_g[l], sh2x, sc2x)
        if not last:
            o_da_y = diff_core(q1y, q2y, k1y, k2y, vy, lam)
            o_mla_y = mla_core(qny, qry, kny, kry, vmy)
            mix_y = merge(o_da_y, o_mla_y, gy, da_subln_g[l], lam_init, w_br_a[l], w_br_b[l], w_out[l])
            y = y + g1y * mix_y
            h2y = modulate(y, norm2_g[l], sh2y, sc2y)
            Ty = y.shape[1]
            f = moe(jnp.concatenate([h2x.reshape(-1, D), h2y.reshape(-1, D)], axis=0), *moe_p)
            x = x + g2x * f[:B * S].reshape(B, S, D)
            y = y + g2y * f[B * S:].reshape(B, Ty, D)
        else:
            x = x + g2x * moe(h2x.reshape(-1, D), *moe_p).reshape(B, S, D)
    return rms(x, final_g)
```

```python
import functools
import math

import jax
import jax.numpy as jnp
from jax import lax
from jax.experimental import pallas as pl
from jax.experimental.pallas import tpu as pltpu

F32 = jnp.float32
BF16 = jnp.bfloat16

EPS = 1e-6
GRID_W = 64
ROPE_BASE = 10000.0
DA_HEADS = 4
DA_HD = 64
MLA_HEADS = 4
MLA_NOPE = 128
MLA_ROPE = 64
MLA_V = 128
MLA_Q_RANK = 384
MLA_KV_RANK = 256
N_EXPERTS = 32
TOP_K = 4
D_FF = 512
SWIGLU_LIMIT = 7.0
SWIGLU_ALPHA = 1.702
EXPERT_BLOCK = 512

HEAD_W = 128
ROW_TILE = 256
KEY_CHUNK = 512
VMEM_LIMIT = 56 * 1024 * 1024


def _cparams(sem):
    return pltpu.CompilerParams(dimension_semantics=sem, vmem_limit_bytes=VMEM_LIMIT)


def _rms_rows(x):
    return x * lax.rsqrt(jnp.mean(x * x, axis=-1, keepdims=True) + EPS)


def _ada_kernel(c_ref, w_ref, b_ref, o_ref):
    c = c_ref[...]
    s = (c * jax.nn.sigmoid(c)).astype(BF16)
    o_ref[0] = jnp.dot(s, w_ref[0].astype(BF16), preferred_element_type=F32) + b_ref[0]


def _ada_all(cc, ada_w, ada_b):
    L, D, D6 = ada_w.shape
    R = cc.shape[0]
    nj = D6 // D
    return pl.pallas_call(
        _ada_kernel,
        out_shape=jax.ShapeDtypeStruct((L, R, D6), F32),
        grid=(L, nj),
        in_specs=[
            pl.BlockSpec((R, D), lambda l, j: (0, 0)),
            pl.BlockSpec((1, D, D), lambda l, j: (l, 0, j)),
            pl.BlockSpec((1, 1, D), lambda l, j: (l, 0, j)),
        ],
        out_specs=pl.BlockSpec((1, R, D), lambda l, j: (l, 0, j)),
        compiler_params=_cparams(("arbitrary", "arbitrary")),
        name="ada_mod",
    )(cc, ada_w, ada_b.reshape(L, 1, D6))


def _rope(z, cos, sin_a, sin_b):
    return z * cos + pltpu.roll(z, 112, 1) * sin_a + pltpu.roll(z, 16, 1) * sin_b


def _pre_kernel(x_ref, mod_ref, g_ref, cos_ref, sa_ref, sb_ref, wm_ref, wkr_ref, wg_ref,
                qag_ref, kvag_ref, wqb_ref, wkvb_ref,
                daq_ref, dak_ref, dav_ref, mq_ref, mk_ref, mv_ref, gate_ref, *, da_scale, mla_scale):
    x = x_ref[0]
    shift = mod_ref[0, 0:1, :]
    scale = mod_ref[0, 1:2, :]
    h = ((_rms_rows(x) * g_ref[...]) * (1.0 + scale) + shift).astype(BF16)
    cos = cos_ref[...]
    sa = sa_ref[...]
    sb = sb_ref[...]
    nh = DA_HEADS
    qw = nh * HEAD_W

    zq = jnp.dot(h, wm_ref[:, 0:qw], preferred_element_type=F32)
    for hd in range(nh):
        z = _rope(zq[:, hd * HEAD_W:(hd + 1) * HEAD_W], cos, sa, sb) * da_scale
        daq_ref[0, hd] = z.astype(BF16)
    zk = jnp.dot(h, wm_ref[:, qw:2 * qw], preferred_element_type=F32)
    for hd in range(nh):
        z = _rope(zk[:, hd * HEAD_W:(hd + 1) * HEAD_W], cos, sa, sb)
        dak_ref[0, hd] = z.astype(BF16)
    zv = jnp.dot(h, wm_ref[:, 2 * qw:3 * qw], preferred_element_type=F32)
    for hd in range(nh):
        dav_ref[0, hd] = zv[:, hd * HEAD_W:(hd + 1) * HEAD_W].astype(BF16)

    o0 = 3 * qw
    cq = jnp.dot(h, wm_ref[:, o0:o0 + MLA_Q_RANK], preferred_element_type=F32)
    cqn = (_rms_rows(cq) * qag_ref[...]).astype(BF16)
    qm = jnp.dot(cqn, wqb_ref[...], preferred_element_type=F32)
    for hd in range(MLA_HEADS):
        mq_ref[0, hd, :, 0:HEAD_W] = (qm[:, hd * HEAD_W:(hd + 1) * HEAD_W] * mla_scale).astype(BF16)
        zr = qm[:, (MLA_HEADS + hd) * HEAD_W:(MLA_HEADS + hd + 1) * HEAD_W]
        mq_ref[0, hd, :, HEAD_W:2 * HEAD_W] = (_rope(zr, cos, sa, sb) * mla_scale).astype(BF16)

    o1 = o0 + MLA_Q_RANK
    ckv = jnp.dot(h, wm_ref[:, o1:o1 + MLA_KV_RANK], preferred_element_type=F32)
    ckvn = (_rms_rows(ckv) * kvag_ref[...]).astype(BF16)
    kv = jnp.dot(ckvn, wkvb_ref[...], preferred_element_type=F32)
    zkr = jnp.dot(h, wkr_ref[...], preferred_element_type=F32)
    kr = _rope(zkr, cos, sa, sb).astype(BF16)
    for hd in range(MLA_HEADS):
        mk_ref[0, hd, :, 0:HEAD_W] = kv[:, hd * HEAD_W:(hd + 1) * HEAD_W].astype(BF16)
        mk_ref[0, hd, :, HEAD_W:2 * HEAD_W] = kr
        mv_ref[0, hd] = kv[:, (MLA_HEADS + hd) * HEAD_W:(MLA_HEADS + hd + 1) * HEAD_W].astype(BF16)

    gz = jnp.dot(h, wg_ref[...], preferred_element_type=F32)
    gate_ref[0] = jax.nn.sigmoid(gz).astype(BF16)


def _pre(xy, mod, g1, cos, sa, sb, wm, wkr, wg, qag, kvag, wqb, wkvb, n_x_tiles):
    B, SA, D = xy.shape
    nt = SA // ROW_TILE
    H = DA_HEADS
    const2 = lambda b, i: (0, 0)
    head_spec = lambda w: pl.BlockSpec((1, H, ROW_TILE, w), lambda b, i: (b, 0, i, 0))
    row_of = lambda b, i: jnp.where(i < n_x_tiles, b, B)
    kern = functools.partial(_pre_kernel, da_scale=1.0 / math.sqrt(DA_HD),
                             mla_scale=1.0 / math.sqrt(MLA_NOPE + MLA_ROPE))
    return pl.pallas_call(
        kern,
        out_shape=(
            jax.ShapeDtypeStruct((B, H, SA, HEAD_W), BF16),
            jax.ShapeDtypeStruct((B, H, SA, HEAD_W), BF16),
            jax.ShapeDtypeStruct((B, H, SA, HEAD_W), BF16),
            jax.ShapeDtypeStruct((B, H, SA, 2 * HEAD_W), BF16),
            jax.ShapeDtypeStruct((B, H, SA, 2 * HEAD_W), BF16),
            jax.ShapeDtypeStruct((B, H, SA, HEAD_W), BF16),
            jax.ShapeDtypeStruct((B, SA, 2 * D), BF16),
        ),
        grid=(B, nt),
        in_specs=[
            pl.BlockSpec((1, ROW_TILE, D), lambda b, i: (b, i, 0)),
            pl.BlockSpec((1, 6, D), lambda b, i: (row_of(b, i), 0, 0)),
            pl.BlockSpec((1, D), const2),
            pl.BlockSpec((ROW_TILE, HEAD_W), lambda b, i: (i, 0)),
            pl.BlockSpec((ROW_TILE, HEAD_W), lambda b, i: (i, 0)),
            pl.BlockSpec((ROW_TILE, HEAD_W), lambda b, i: (i, 0)),
            pl.BlockSpec(wm.shape, const2),
            pl.BlockSpec(wkr.shape, const2),
            pl.BlockSpec(wg.shape, const2),
            pl.BlockSpec(qag.shape, const2),
            pl.BlockSpec(kvag.shape, const2),
            pl.BlockSpec(wqb.shape, const2),
            pl.BlockSpec(wkvb.shape, const2),
        ],
        out_specs=(
            head_spec(HEAD_W), head_spec(HEAD_W), head_spec(HEAD_W),
            head_spec(2 * HEAD_W), head_spec(2 * HEAD_W), head_spec(HEAD_W),
            pl.BlockSpec((1, ROW_TILE, 2 * D), lambda b, i: (b, i, 0)),
        ),
        compiler_params=_cparams(("parallel", "arbitrary")),
        name="pre_mixer",
    )(xy, mod, g1, cos, sa, sb, wm, wkr, wg, qag, kvag, wqb, wkvb)


def _key_chunks(start, stop):
    out = []
    while start < stop:
        size = min(KEY_CHUNK, stop - start)
        out.append((start, size))
        start += size
    return out


def _softmax_pv(qq, k_ref, v_ref, s_ref, chunks):
    m = None
    for (st, sz) in chunks:
        s = lax.dot_general(qq, k_ref[0, 0, st:st + sz, :], (((1,), (1,)), ((), ())),
                            preferred_element_type=F32)
        s_ref[:, st:st + sz] = s
        mc = jnp.max(s, axis=-1, keepdims=True)
        m = mc if m is None else jnp.maximum(m, mc)
    l = None
    acc = None
    for (st, sz) in chunks:
        e = jnp.exp(s_ref[:, st:st + sz] - m)
        lc = jnp.sum(e, axis=-1, keepdims=True)
        pc = jnp.dot(e.astype(BF16), v_ref[0, 0, st:st + sz, :], preferred_element_type=F32)
        l = lc if l is None else l + lc
        acc = pc if acc is None else acc + pc
    return acc / l


def _da_kernel(sc_ref, q_ref, k_ref, v_ref, g_ref, o_ref, s_ref, *, n_x_tiles, n_x_keys, n_keys):
    i = pl.program_id(2)
    tq = q_ref.shape[2]

    def run(chunks):
        q = q_ref[0, 0]
        lane = lax.broadcasted_iota(jnp.int32, q.shape, 1)
        zero = jnp.zeros_like(q)
        qq = jnp.concatenate([jnp.where(lane < DA_HD, q, zero), jnp.where(lane >= DA_HD, q, zero)], axis=0)
        o = _softmax_pv(qq, k_ref, v_ref, s_ref, chunks)
        od = o[:tq] - sc_ref[0] * o[tq:]
        od = (_rms_rows(od) * g_ref[...]) * sc_ref[1]
        o_ref[0] = od.astype(BF16)

    @pl.when(i < n_x_tiles)
    def _():
        run(_key_chunks(0, n_keys))

    @pl.when(i >= n_x_tiles)
    def _():
        run(_key_chunks(n_x_keys, n_keys))


def _mla_kernel(q_ref, k_ref, v_ref, o_ref, s_ref, *, n_x_tiles, n_x_keys, n_keys):
    i = pl.program_id(2)

    def run(chunks):
        o_ref[0] = _softmax_pv(q_ref[0, 0], k_ref, v_ref, s_ref, chunks).astype(BF16)

    @pl.when(i < n_x_tiles)
    def _():
        run(_key_chunks(0, n_keys))

    @pl.when(i >= n_x_tiles)
    def _():
        run(_key_chunks(n_x_keys, n_keys))


def _attention(kind, q, k, v, n_x_tiles, n_q_tiles, scal=None, g=None):
    B, H, SA, QW = q.shape
    tq = ROW_TILE
    n_x_keys = n_x_tiles * ROW_TILE
    kw = dict(n_x_tiles=n_x_tiles, n_x_keys=n_x_keys, n_keys=SA)
    q_spec = pl.BlockSpec((1, 1, tq, QW), lambda b, h, i: (b, h, i, 0))
    k_spec = pl.BlockSpec((1, 1, SA, QW), lambda b, h, i: (b, h, 0, 0))
    v_spec = pl.BlockSpec((1, 1, SA, HEAD_W), lambda b, h, i: (b, h, 0, 0))
    o_spec = pl.BlockSpec((1, tq, HEAD_W), lambda b, h, i: (b, i, h))
    out_shape = jax.ShapeDtypeStruct((B, n_q_tiles * tq, H * HEAD_W), BF16)
    sem = ("parallel", "parallel", "arbitrary")
    if kind == "da":
        return pl.pallas_call(
            functools.partial(_da_kernel, **kw),
            out_shape=out_shape,
            grid=(B, H, n_q_tiles),
            in_specs=[pl.BlockSpec(memory_space=pltpu.SMEM), q_spec, k_spec, v_spec,
                      pl.BlockSpec((1, HEAD_W), lambda b, h, i: (0, 0))],
            out_specs=o_spec,
            scratch_shapes=[pltpu.VMEM((2 * tq, SA), F32)],
            compiler_params=_cparams(sem),
            name="diff_attention",
        )(scal, q, k, v, g)
    return pl.pallas_call(
        functools.partial(_mla_kernel, **kw),
        out_shape=out_shape,
        grid=(B, H, n_q_tiles),
        in_specs=[q_spec, k_spec, v_spec],
        out_specs=o_spec,
        scratch_shapes=[pltpu.VMEM((tq, SA), F32)],
        compiler_params=_cparams(sem),
        name="mla_attention",
    )(q, k, v)


def _merge_kernel(x_ref, oda_ref, omla_ref, gate_ref, mod_ref, g2_ref, wa_ref, wb_ref, wo_ref,
                  rw_ref, rb_ref, xo_ref, h2_ref, lg_ref):
    D = x_ref.shape[2]
    a = jnp.dot(oda_ref[0], wa_ref[...], preferred_element_type=F32)
    b = jnp.dot(omla_ref[0], wb_ref[...], preferred_element_type=F32)
    ga = gate_ref[0, :, 0:D].astype(F32)
    gb = gate_ref[0, :, D:2 * D].astype(F32)
    m = (ga * a + gb * b).astype(BF16)
    mix = jnp.dot(m, wo_ref[...], preferred_element_type=F32)
    xn = x_ref[0] + mod_ref[0, 2:3, :] * mix
    xo_ref[0] = xn
    h2 = (_rms_rows(xn) * g2_ref[...]) * (1.0 + mod_ref[0, 4:5, :]) + mod_ref[0, 3:4, :]
    h2b = h2.astype(BF16)
    h2_ref[0] = h2b
    lg_ref[0] = jnp.dot(h2b, rw_ref[...], preferred_element_type=F32) + rb_ref[...]


def _merge(xy, oda, omla, gates, mod, g2, wa, wb, wo, rw, rb, n_x_tiles, n_q_tiles):
    B, SA, D = xy.shape
    const2 = lambda b, i: (0, 0)
    row_of = lambda b, i: jnp.where(i < n_x_tiles, b, B)
    row_spec = lambda w: pl.BlockSpec((1, ROW_TILE, w), lambda b, i: (b, i, 0))
    RW = rw.shape[1]
    SQ = n_q_tiles * ROW_TILE
    return pl.pallas_call(
        _merge_kernel,
        out_shape=(
            jax.ShapeDtypeStruct((B, SQ, D), F32),
            jax.ShapeDtypeStruct((B, SQ, D), BF16),
            jax.ShapeDtypeStruct((B, SQ, RW), F32),
        ),
        grid=(B, n_q_tiles),
        in_specs=[
            row_spec(D), row_spec(oda.shape[2]), row_spec(omla.shape[2]), row_spec(2 * D),
            pl.BlockSpec((1, 6, D), lambda b, i: (row_of(b, i), 0, 0)),
            pl.BlockSpec((1, D), const2),
            pl.BlockSpec(wa.shape, const2), pl.BlockSpec(wb.shape, const2), pl.BlockSpec(wo.shape, const2),
            pl.BlockSpec(rw.shape, const2), pl.BlockSpec(rb.shape, const2),
        ],
        out_specs=(row_spec(D), row_spec(D), row_spec(RW)),
        compiler_params=_cparams(("parallel", "arbitrary")),
        name="merge_post",
    )(xy, oda, omla, gates, mod, g2, wa, wb, wo, rw, rb)


def _expert_kernel(be_ref, nu_ref, x_ref, wgu_ref, bgu_ref, wdn_ref, bdn_ref, o_ref):
    n = pl.program_id(0)

    @pl.when(n < nu_ref[0])
    def _():
        h = jnp.dot(x_ref[...], wgu_ref[0], preferred_element_type=F32) + bgu_ref[0]
        gate = jnp.minimum(h[:, :D_FF], SWIGLU_LIMIT)
        lin = jnp.clip(h[:, D_FF:], -SWIGLU_LIMIT, SWIGLU_LIMIT)
        act = (lin + 1.0) * gate * jax.nn.sigmoid(SWIGLU_ALPHA * gate)
        o_ref[...] = jnp.dot(act.astype(BF16), wdn_ref[0], preferred_element_type=F32) + bdn_ref[0]


def _experts(block_e, n_used, buf, wgu, bgu, wdn, bdn):
    R, D = buf.shape
    NB = R // EXPERT_BLOCK
    E = wgu.shape[0]
    return pl.pallas_call(
        _expert_kernel,
        out_shape=jax.ShapeDtypeStruct((R, D), F32),
        grid_spec=pltpu.PrefetchScalarGridSpec(
            num_scalar_prefetch=2,
            grid=(NB,),
            in_specs=[
                pl.BlockSpec((EXPERT_BLOCK, D), lambda n, be, nu: (jnp.minimum(n, nu[0] - 1), 0)),
                pl.BlockSpec((1, D, 2 * D_FF), lambda n, be, nu: (be[n], 0, 0)),
                pl.BlockSpec((1, 1, 2 * D_FF), lambda n, be, nu: (be[n], 0, 0)),
                pl.BlockSpec((1, D_FF, D), lambda n, be, nu: (be[n], 0, 0)),
                pl.BlockSpec((1, 1, D), lambda n, be, nu: (be[n], 0, 0)),
            ],
            out_specs=pl.BlockSpec((EXPERT_BLOCK, D), lambda n, be, nu: (n, 0)),
        ),
        compiler_params=_cparams(("arbitrary",)),
        name="expert_mlp",
    )(block_e, n_used, buf, wgu, bgu.reshape(E, 1, 2 * D_FF), wdn, bdn.reshape(E, 1, D))


def _combine_kernel(x_ref, y_ref, w_ref, mod_ref, fg_ref, o_ref, *, final):
    w = w_ref[0]
    f = w[:, 0:1] * y_ref[0, 0]
    for k in range(1, TOP_K):
        f = f + w[:, k:k + 1] * y_ref[k, 0]
    xn = x_ref[0] + mod_ref[0, 5:6, :] * f
    if final:
        xn = _rms_rows(xn) * fg_ref[...]
    o_ref[0] = xn


def _combine(xy, yk, wts, mod, fg, n_x_tiles, n_q_tiles, final):
    B, SA, D = xy.shape
    SQ = n_q_tiles * ROW_TILE
    row_of = lambda b, i: jnp.where(i < n_x_tiles, b, B)
    return pl.pallas_call(
        functools.partial(_combine_kernel, final=final),
        out_shape=jax.ShapeDtypeStruct((B, SQ, D), F32),
        grid=(B, n_q_tiles),
        in_specs=[
            pl.BlockSpec((1, ROW_TILE, D), lambda b, i: (b, i, 0)),
            pl.BlockSpec((TOP_K, 1, ROW_TILE, D), lambda b, i: (0, b, i, 0)),
            pl.BlockSpec((1, ROW_TILE, TOP_K), lambda b, i: (b, i, 0)),
            pl.BlockSpec((1, 6, D), lambda b, i: (row_of(b, i), 0, 0)),
            pl.BlockSpec((1, D), lambda b, i: (0, 0)),
        ],
        out_specs=pl.BlockSpec((1, ROW_TILE, D), lambda b, i: (b, i, 0)),
        compiler_params=_cparams(("parallel", "arbitrary")),
        name="moe_combine",
    )(xy, yk, wts, mod, fg)


def _route(logits):
    T = logits.shape[0]
    TK = T * TOP_K
    NB = (TK + EXPERT_BLOCK - 1) // EXPERT_BLOCK + N_EXPERTS
    topv, topi = lax.top_k(logits, TOP_K)
    wts = jax.nn.softmax(topv, axis=-1)
    flat_e = topi.reshape(-1).astype(jnp.int32)
    order = jnp.argsort(flat_e)
    e_sorted = flat_e[order]
    sizes = jnp.bincount(flat_e, length=N_EXPERTS).astype(jnp.int32)
    padded = ((sizes + EXPERT_BLOCK - 1) // EXPERT_BLOCK) * EXPERT_BLOCK
    start = jnp.cumsum(sizes) - sizes
    pend = jnp.cumsum(padded)
    pstart = pend - padded
    dest = pstart[e_sorted] + (jnp.arange(TK, dtype=jnp.int32) - start[e_sorted])
    src_tok = jnp.zeros((NB * EXPERT_BLOCK,), jnp.int32).at[dest].set((order // TOP_K).astype(jnp.int32))
    slot = jnp.zeros((TK,), jnp.int32).at[order].set(dest)
    block_e = jnp.minimum(jnp.searchsorted(pend, jnp.arange(NB, dtype=jnp.int32) * EXPERT_BLOCK, side='right'),
                          N_EXPERTS - 1).astype(jnp.int32)
    n_used = (pend[-1] // EXPERT_BLOCK).astype(jnp.int32).reshape(1)
    return src_tok, slot.reshape(T, TOP_K), wts, block_e, n_used


def _rope_tables(S, n_ctx):
    rows = jnp.repeat(jnp.arange(S // GRID_W, dtype=F32), GRID_W)
    cols = jnp.tile(jnp.arange(GRID_W, dtype=F32), S // GRID_W)
    half = DA_HD // 2
    freqs = ROPE_BASE ** (-jnp.arange(0, half, 2, dtype=F32) / half)
    ar = rows[:, None] * freqs
    ac = cols[:, None] * freqs
    ang = jnp.concatenate([ar, ar, ac, ac], axis=-1)
    cos = jnp.concatenate([jnp.cos(ang), jnp.ones((n_ctx, DA_HD), F32)], axis=0)
    sin = jnp.concatenate([jnp.sin(ang), jnp.zeros((n_ctx, DA_HD), F32)], axis=0)
    cos = jnp.tile(cos, (1, HEAD_W // DA_HD))
    sin = jnp.tile(sin, (1, HEAD_W // DA_HD))
    lo = (jnp.arange(HEAD_W) % (DA_HD // 2)) < (DA_HD // 4)
    return cos, jnp.where(lo, -sin, 0.0), jnp.where(lo, 0.0, sin)


def kernel(x, c, ctx, c_ctx, ada_w, ada_b, norm1_g, norm2_g, w_in, da_lq1, da_lk1, da_lq2, da_lk2, da_subln_g, mla_qa_g, mla_kva_g, mla_wqb, mla_wkvb, w_br_a, w_br_b, w_out, router_w, router_b, moe_wgu, moe_bgu, moe_wdn, moe_bdn, final_g):
    B, S, D = x.shape
    CT = ctx.shape[1]
    L = ada_w.shape[0]
    SA = S + CT
    assert S % ROW_TILE == 0 and CT == ROW_TILE and S % GRID_W == 0
    n_x_tiles = S // ROW_TILE
    n_tiles = SA // ROW_TILE
    H = DA_HEADS

    xy = jnp.concatenate([x, ctx], axis=1)
    cc = jnp.concatenate([c, c_ctx[None, :], jnp.zeros((16 - B - 1, D), F32)], axis=0)
    mod_all = _ada_all(cc, ada_w, ada_b).reshape(L, 16, 6, D)
    cos, sa, sb = _rope_tables(S, CT)

    qw = H * HEAD_W
    o_cq = 3 * qw
    o_kr = o_cq + MLA_Q_RANK + MLA_KV_RANK
    o_g = o_kr + MLA_ROPE
    wm = w_in[:, :, :o_kr].astype(BF16)
    wkr = jnp.pad(w_in[:, :, o_kr:o_g], ((0, 0), (0, 0), (0, HEAD_W - MLA_ROPE))).astype(BF16)
    wg = w_in[:, :, o_g:].astype(BF16)
    wqb4 = mla_wqb.reshape(L, MLA_Q_RANK, MLA_HEADS, MLA_NOPE + MLA_ROPE)
    wqb_n = wqb4[..., :MLA_NOPE].reshape(L, MLA_Q_RANK, MLA_HEADS * MLA_NOPE)
    wqb_r = jnp.pad(wqb4[..., MLA_NOPE:], ((0, 0), (0, 0), (0, 0), (0, HEAD_W - MLA_ROPE)))
    wqb = jnp.concatenate([wqb_n, wqb_r.reshape(L, MLA_Q_RANK, MLA_HEADS * HEAD_W)], axis=-1).astype(BF16)
    wkvb4 = mla_wkvb.reshape(L, MLA_KV_RANK, MLA_HEADS, MLA_NOPE + MLA_V)
    wkvb = jnp.concatenate([wkvb4[..., :MLA_NOPE].reshape(L, MLA_KV_RANK, -1),
                            wkvb4[..., MLA_NOPE:].reshape(L, MLA_KV_RANK, -1)], axis=-1).astype(BF16)
    wa = w_br_a.astype(BF16)
    wb = w_br_b.astype(BF16)
    wo = w_out.astype(BF16)
    RW = 128
    rw = jnp.pad(router_w, ((0, 0), (0, 0), (0, RW - N_EXPERTS))).astype(BF16)
    rb = jnp.pad(router_b, ((0, 0), (0, RW - N_EXPERTS)))
    wgu = moe_wgu.astype(BF16)
    wdn = moe_wdn.astype(BF16)

    for l in range(L):
        last = l == L - 1
        n_q = n_x_tiles if last else n_tiles
        mod = mod_all[l]
        lam_init = 0.8 - 0.6 * math.exp(-0.3 * l)
        lam = (jnp.exp(jnp.sum(da_lq1[l] * da_lk1[l])) - jnp.exp(jnp.sum(da_lq2[l] * da_lk2[l])) + lam_init)
        scal = jnp.stack([lam, jnp.asarray(1.0 - lam_init, F32)]).astype(F32)

        daq, dak, dav, mq, mk, mv, gates = _pre(
            xy, mod, norm1_g[l][None], cos, sa, sb, wm[l], wkr[l], wg[l],
            mla_qa_g[l][None], mla_kva_g[l][None], wqb[l], wkvb[l], n_x_tiles)
        oda = _attention("da", daq, dak, dav, n_x_tiles, n_q, scal=scal, g=da_subln_g[l][None])
        omla = _attention("mla", mq, mk, mv, n_x_tiles, n_q)
        xy2, h2, logits = _merge(xy, oda, omla, gates, mod, norm2_g[l][None], wa[l], wb[l], wo[l],
                                 rw[l], rb[l][None], n_x_tiles, n_q)

        SQ = n_q * ROW_TILE
        T = B * SQ
        src_tok, slot, wts, block_e, n_used = _route(logits[:, :, :N_EXPERTS].reshape(T, N_EXPERTS))
        buf = h2.reshape(T, D)[src_tok]
        out_sorted = _experts(block_e, n_used, buf, wgu[l], moe_bgu[l], wdn[l], moe_bdn[l])
        yk = out_sorted[slot.T.reshape(-1)].reshape(TOP_K, B, SQ, D)
        xy = _combine(xy2, yk, wts.reshape(B, SQ, TOP_K), mod, final_g[None], n_x_tiles, n_q, last)
    return xy
```

```python
import functools
import math

import jax
import jax.numpy as jnp
from jax import lax
from jax.experimental import pallas as pl
from jax.experimental.pallas import tpu as pltpu

F32 = jnp.float32
BF16 = jnp.bfloat16

EPS = 1e-6
GRID_W = 64
ROPE_BASE = 10000.0
DA_HEADS = 4
DA_HD = 64
MLA_HEADS = 4
MLA_NOPE = 128
MLA_ROPE = 64
MLA_V = 128
MLA_Q_RANK = 384
MLA_KV_RANK = 256
N_EXPERTS = 32
TOP_K = 4
D_FF = 512
SWIGLU_LIMIT = 7.0
SWIGLU_ALPHA = 1.702
EXPERT_BLOCK = 512

LOG2E = math.log2(math.e)
HEAD_W = 128
ROW_TILE = 256
KEY_CHUNK = 512
VMEM_LIMIT = 56 * 1024 * 1024


def _cparams(sem):
    return pltpu.CompilerParams(dimension_semantics=sem, vmem_limit_bytes=VMEM_LIMIT)


def _rms_rows(x):
    return x * lax.rsqrt(jnp.mean(x * x, axis=-1, keepdims=True) + EPS)


def _ada_kernel(c_ref, w_ref, b_ref, o_ref):
    c = c_ref[...]
    s = (c * jax.nn.sigmoid(c)).astype(BF16)
    o_ref[0] = jnp.dot(s, w_ref[0].astype(BF16), preferred_element_type=F32) + b_ref[0]


def _ada_all(cc, ada_w, ada_b):
    L, D, D6 = ada_w.shape
    R = cc.shape[0]
    nj = D6 // D
    return pl.pallas_call(
        _ada_kernel,
        out_shape=jax.ShapeDtypeStruct((L, R, D6), F32),
        grid=(L, nj),
        in_specs=[
            pl.BlockSpec((R, D), lambda l, j: (0, 0)),
            pl.BlockSpec((1, D, D), lambda l, j: (l, 0, j)),
            pl.BlockSpec((1, 1, D), lambda l, j: (l, 0, j)),
        ],
        out_specs=pl.BlockSpec((1, R, D), lambda l, j: (l, 0, j)),
        compiler_params=_cparams(("arbitrary", "arbitrary")),
        name="ada_mod",
    )(cc, ada_w, ada_b.reshape(L, 1, D6))


def _rope(z, cos, sin_a, sin_b):
    return z * cos + pltpu.roll(z, 112, 1) * sin_a + pltpu.roll(z, 16, 1) * sin_b


def _pre_kernel(x_ref, mod_ref, g_ref, cos_ref, sa_ref, sb_ref, wm_ref, wkr_ref, wg_ref,
                qag_ref, kvag_ref, wqb_ref, wkvb_ref,
                daq_ref, dak_ref, dav_ref, mq_ref, mk_ref, mv_ref, gate_ref, *, da_scale, mla_scale):
    x = x_ref[0]
    shift = mod_ref[0, 0:1, :]
    scale = mod_ref[0, 1:2, :]
    h = ((_rms_rows(x) * g_ref[...]) * (1.0 + scale) + shift).astype(BF16)
    cos = cos_ref[...]
    sa = sa_ref[...]
    sb = sb_ref[...]
    nh = DA_HEADS
    qw = nh * HEAD_W

    zq = jnp.dot(h, wm_ref[:, 0:qw], preferred_element_type=F32)
    for hd in range(nh):
        z = _rope(zq[:, hd * HEAD_W:(hd + 1) * HEAD_W], cos, sa, sb) * da_scale
        daq_ref[0, hd] = z.astype(BF16)
    zk = jnp.dot(h, wm_ref[:, qw:2 * qw], preferred_element_type=F32)
    for hd in range(nh):
        z = _rope(zk[:, hd * HEAD_W:(hd + 1) * HEAD_W], cos, sa, sb)
        dak_ref[0, hd] = z.astype(BF16)
    zv = jnp.dot(h, wm_ref[:, 2 * qw:3 * qw], preferred_element_type=F32)
    for hd in range(nh):
        dav_ref[0, hd] = zv[:, hd * HEAD_W:(hd + 1) * HEAD_W].astype(BF16)

    o0 = 3 * qw
    cq = jnp.dot(h, wm_ref[:, o0:o0 + MLA_Q_RANK], preferred_element_type=F32)
    cqn = (_rms_rows(cq) * qag_ref[...]).astype(BF16)
    qm = jnp.dot(cqn, wqb_ref[...], preferred_element_type=F32)
    for hd in range(MLA_HEADS):
        mq_ref[0, hd, :, 0:HEAD_W] = (qm[:, hd * HEAD_W:(hd + 1) * HEAD_W] * mla_scale).astype(BF16)
        zr = qm[:, (MLA_HEADS + hd) * HEAD_W:(MLA_HEADS + hd + 1) * HEAD_W]
        mq_ref[0, hd, :, HEAD_W:2 * HEAD_W] = (_rope(zr, cos, sa, sb) * mla_scale).astype(BF16)

    o1 = o0 + MLA_Q_RANK
    ckv = jnp.dot(h, wm_ref[:, o1:o1 + MLA_KV_RANK], preferred_element_type=F32)
    ckvn = (_rms_rows(ckv) * kvag_ref[...]).astype(BF16)
    kv = jnp.dot(ckvn, wkvb_ref[...], preferred_element_type=F32)
    zkr = jnp.dot(h, wkr_ref[...], preferred_element_type=F32)
    kr = _rope(zkr, cos, sa, sb).astype(BF16)
    for hd in range(MLA_HEADS):
        mk_ref[0, hd, :, 0:HEAD_W] = kv[:, hd * HEAD_W:(hd + 1) * HEAD_W].astype(BF16)
        mk_ref[0, hd, :, HEAD_W:2 * HEAD_W] = kr
        mv_ref[0, hd] = kv[:, (MLA_HEADS + hd) * HEAD_W:(MLA_HEADS + hd + 1) * HEAD_W].astype(BF16)

    gz = jnp.dot(h, wg_ref[...], preferred_element_type=F32)
    gate_ref[0] = jax.nn.sigmoid(gz).astype(BF16)


def _pre(xy, mod, g1, cos, sa, sb, wm, wkr, wg, qag, kvag, wqb, wkvb, n_x_tiles):
    B, SA, D = xy.shape
    nt = SA // ROW_TILE
    H = DA_HEADS
    const2 = lambda b, i: (0, 0)
    head_spec = lambda w: pl.BlockSpec((1, H, ROW_TILE, w), lambda b, i: (b, 0, i, 0))
    row_of = lambda b, i: jnp.where(i < n_x_tiles, b, B)
    kern = functools.partial(_pre_kernel, da_scale=LOG2E / math.sqrt(DA_HD),
                             mla_scale=LOG2E / math.sqrt(MLA_NOPE + MLA_ROPE))
    return pl.pallas_call(
        kern,
        out_shape=(
            jax.ShapeDtypeStruct((B, H, SA, HEAD_W), BF16),
            jax.ShapeDtypeStruct((B, H, SA, HEAD_W), BF16),
            jax.ShapeDtypeStruct((B, H, SA, HEAD_W), BF16),
            jax.ShapeDtypeStruct((B, H, SA, 2 * HEAD_W), BF16),
            jax.ShapeDtypeStruct((B, H, SA, 2 * HEAD_W), BF16),
            jax.ShapeDtypeStruct((B, H, SA, HEAD_W), BF16),
            jax.ShapeDtypeStruct((B, SA, 2 * D), BF16),
        ),
        grid=(B, nt),
        in_specs=[
            pl.BlockSpec((1, ROW_TILE, D), lambda b, i: (b, i, 0)),
            pl.BlockSpec((1, 6, D), lambda b, i: (row_of(b, i), 0, 0)),
            pl.BlockSpec((1, D), const2),
            pl.BlockSpec((ROW_TILE, HEAD_W), lambda b, i: (i, 0)),
            pl.BlockSpec((ROW_TILE, HEAD_W), lambda b, i: (i, 0)),
            pl.BlockSpec((ROW_TILE, HEAD_W), lambda b, i: (i, 0)),
            pl.BlockSpec(wm.shape, const2),
            pl.BlockSpec(wkr.shape, const2),
            pl.BlockSpec(wg.shape, const2),
            pl.BlockSpec(qag.shape, const2),
            pl.BlockSpec(kvag.shape, const2),
            pl.BlockSpec(wqb.shape, const2),
            pl.BlockSpec(wkvb.shape, const2),
        ],
        out_specs=(
            head_spec(HEAD_W), head_spec(HEAD_W), head_spec(HEAD_W),
            head_spec(2 * HEAD_W), head_spec(2 * HEAD_W), head_spec(HEAD_W),
            pl.BlockSpec((1, ROW_TILE, 2 * D), lambda b, i: (b, i, 0)),
        ),
        compiler_params=_cparams(("parallel", "arbitrary")),
        name="pre_mixer",
    )(xy, mod, g1, cos, sa, sb, wm, wkr, wg, qag, kvag, wqb, wkvb)


def _key_chunks(start, stop):
    out = []
    while start < stop:
        size = min(KEY_CHUNK, stop - start)
        out.append((start, size))
        start += size
    return out


def _fill_value_ext(v_ref, vx_ref):
    vx_ref[:, 0:HEAD_W] = v_ref[0, 0]
    vx_ref[:, HEAD_W:2 * HEAD_W] = jnp.ones((v_ref.shape[2], HEAD_W), BF16)


def _flash(qq, k_ref, vx_ref, chunks):
    m = None
    acc = None
    for (st, sz) in chunks:
        s = lax.dot_general(qq, k_ref[0, 0, st:st + sz, :], (((1,), (1,)), ((), ())),
                            preferred_element_type=F32)
        mc = jnp.max(s, axis=-1, keepdims=True)
        m_new = mc if m is None else jnp.maximum(m, mc)
        e = jnp.exp2(s - m_new)
        pc = jnp.dot(e.astype(BF16), vx_ref[st:st + sz, :], preferred_element_type=F32)
        acc = pc if m is None else jnp.exp2(m - m_new) * acc + pc
        m = m_new
    return acc[:, :HEAD_W] / acc[:, HEAD_W:HEAD_W + 1]


def _da_kernel(sc_ref, q_ref, k_ref, v_ref, g_ref, o_ref, vx_ref, *, n_x_tiles, n_x_keys, n_keys):
    i = pl.program_id(2)
    tq = q_ref.shape[2]

    @pl.when(i == 0)
    def _():
        _fill_value_ext(v_ref, vx_ref)

    def run(chunks):
        q = q_ref[0, 0]
        lane = lax.broadcasted_iota(jnp.int32, q.shape, 1)
        zero = jnp.zeros_like(q)
        qq = jnp.concatenate([jnp.where(lane < DA_HD, q, zero), jnp.where(lane >= DA_HD, q, zero)], axis=0)
        o = _flash(qq, k_ref, vx_ref, chunks)
        od = o[:tq] - sc_ref[0] * o[tq:]
        od = (_rms_rows(od) * g_ref[...]) * sc_ref[1]
        o_ref[0] = od.astype(BF16)

    @pl.when(i < n_x_tiles)
    def _():
        run(_key_chunks(0, n_keys))

    @pl.when(i >= n_x_tiles)
    def _():
        run(_key_chunks(n_x_keys, n_keys))


def _mla_kernel(q_ref, k_ref, v_ref, o_ref, vx_ref, *, n_x_tiles, n_x_keys, n_keys):
    i = pl.program_id(2)

    @pl.when(i == 0)
    def _():
        _fill_value_ext(v_ref, vx_ref)

    def run(chunks):
        o_ref[0] = _flash(q_ref[0, 0], k_ref, vx_ref, chunks).astype(BF16)

    @pl.when(i < n_x_tiles)
    def _():
        run(_key_chunks(0, n_keys))

    @pl.when(i >= n_x_tiles)
    def _():
        run(_key_chunks(n_x_keys, n_keys))


def _attention(kind, q, k, v, n_x_tiles, n_q_tiles, scal=None, g=None):
    B, H, SA, QW = q.shape
    tq = ROW_TILE
    n_x_keys = n_x_tiles * ROW_TILE
    kw = dict(n_x_tiles=n_x_tiles, n_x_keys=n_x_keys, n_keys=SA)
    q_spec = pl.BlockSpec((1, 1, tq, QW), lambda b, h, i: (b, h, i, 0))
    k_spec = pl.BlockSpec((1, 1, SA, QW), lambda b, h, i: (b, h, 0, 0))
    v_spec = pl.BlockSpec((1, 1, SA, HEAD_W), lambda b, h, i: (b, h, 0, 0))
    o_spec = pl.BlockSpec((1, tq, HEAD_W), lambda b, h, i: (b, i, h))
    out_shape = jax.ShapeDtypeStruct((B, n_q_tiles * tq, H * HEAD_W), BF16)
    sem = ("parallel", "parallel", "arbitrary")
    if kind == "da":
        return pl.pallas_call(
            functools.partial(_da_kernel, **kw),
            out_shape=out_shape,
            grid=(B, H, n_q_tiles),
            in_specs=[pl.BlockSpec(memory_space=pltpu.SMEM), q_spec, k_spec, v_spec,
                      pl.BlockSpec((1, HEAD_W), lambda b, h, i: (0, 0))],
            out_specs=o_spec,
            scratch_shapes=[pltpu.VMEM((SA, 2 * HEAD_W), BF16)],
            compiler_params=_cparams(sem),
            name="diff_attention",
        )(scal, q, k, v, g)
    return pl.pallas_call(
        functools.partial(_mla_kernel, **kw),
        out_shape=out_shape,
        grid=(B, H, n_q_tiles),
        in_specs=[q_spec, k_spec, v_spec],
        out_specs=o_spec,
        scratch_shapes=[pltpu.VMEM((SA, 2 * HEAD_W), BF16)],
        compiler_params=_cparams(sem),
        name="mla_attention",
    )(q, k, v)


def _merge_kernel(x_ref, oda_ref, omla_ref, gate_ref, mod_ref, g2_ref, wa_ref, wb_ref, wo_ref,
                  rw_ref, rb_ref, xo_ref, h2_ref, lg_ref):
    D = x_ref.shape[2]
    a = jnp.dot(oda_ref[0], wa_ref[...], preferred_element_type=F32)
    b = jnp.dot(omla_ref[0], wb_ref[...], preferred_element_type=F32)
    ga = gate_ref[0, :, 0:D].astype(F32)
    gb = gate_ref[0, :, D:2 * D].astype(F32)
    m = (ga * a + gb * b).astype(BF16)
    mix = jnp.dot(m, wo_ref[...], preferred_element_type=F32)
    xn = x_ref[0] + mod_ref[0, 2:3, :] * mix
    xo_ref[0] = xn
    h2 = (_rms_rows(xn) * g2_ref[...]) * (1.0 + mod_ref[0, 4:5, :]) + mod_ref[0, 3:4, :]
    h2b = h2.astype(BF16)
    h2_ref[0] = h2b
    lg_ref[0] = jnp.dot(h2b, rw_ref[...], preferred_element_type=F32) + rb_ref[...]


def _merge(xy, oda, omla, gates, mod, g2, wa, wb, wo, rw, rb, n_x_tiles, n_q_tiles):
    B, SA, D = xy.shape
    const2 = lambda b, i: (0, 0)
    row_of = lambda b, i: jnp.where(i < n_x_tiles, b, B)
    row_spec = lambda w: pl.BlockSpec((1, ROW_TILE, w), lambda b, i: (b, i, 0))
    RW = rw.shape[1]
    SQ = n_q_tiles * ROW_TILE
    return pl.pallas_call(
        _merge_kernel,
        out_shape=(
            jax.ShapeDtypeStruct((B, SQ, D), F32),
            jax.ShapeDtypeStruct((B, SQ, D), BF16),
            jax.ShapeDtypeStruct((B, SQ, RW), F32),
        ),
        grid=(B, n_q_tiles),
        in_specs=[
            row_spec(D), row_spec(oda.shape[2]), row_spec(omla.shape[2]), row_spec(2 * D),
            pl.BlockSpec((1, 6, D), lambda b, i: (row_of(b, i), 0, 0)),
            pl.BlockSpec((1, D), const2),
            pl.BlockSpec(wa.shape, const2), pl.BlockSpec(wb.shape, const2), pl.BlockSpec(wo.shape, const2),
            pl.BlockSpec(rw.shape, const2), pl.BlockSpec(rb.shape, const2),
        ],
        out_specs=(row_spec(D), row_spec(D), row_spec(RW)),
        compiler_params=_cparams(("parallel", "arbitrary")),
        name="merge_post",
    )(xy, oda, omla, gates, mod, g2, wa, wb, wo, rw, rb)


def _expert_kernel(be_ref, nu_ref, x_ref, wgu_ref, bgu_ref, wdn_ref, bdn_ref, o_ref):
    n = pl.program_id(0)

    @pl.when(n < nu_ref[0])
    def _():
        h = jnp.dot(x_ref[...], wgu_ref[0], preferred_element_type=F32) + bgu_ref[0]
        gate = jnp.minimum(h[:, :D_FF], SWIGLU_LIMIT)
        lin = jnp.clip(h[:, D_FF:], -SWIGLU_LIMIT, SWIGLU_LIMIT)
        act = (lin + 1.0) * gate * jax.nn.sigmoid(SWIGLU_ALPHA * gate)
        o_ref[...] = jnp.dot(act.astype(BF16), wdn_ref[0], preferred_element_type=F32) + bdn_ref[0]


def _experts(block_e, n_used, buf, wgu, bgu, wdn, bdn):
    R, D = buf.shape
    NB = R // EXPERT_BLOCK
    E = wgu.shape[0]
    return pl.pallas_call(
        _expert_kernel,
        out_shape=jax.ShapeDtypeStruct((R, D), F32),
        grid_spec=pltpu.PrefetchScalarGridSpec(
            num_scalar_prefetch=2,
            grid=(NB,),
            in_specs=[
                pl.BlockSpec((EXPERT_BLOCK, D), lambda n, be, nu: (jnp.minimum(n, nu[0] - 1), 0)),
                pl.BlockSpec((1, D, 2 * D_FF), lambda n, be, nu: (be[n], 0, 0)),
                pl.BlockSpec((1, 1, 2 * D_FF), lambda n, be, nu: (be[n], 0, 0)),
                pl.BlockSpec((1, D_FF, D), lambda n, be, nu: (be[n], 0, 0)),
                pl.BlockSpec((1, 1, D), lambda n, be, nu: (be[n], 0, 0)),
            ],
            out_specs=pl.BlockSpec((EXPERT_BLOCK, D), lambda n, be, nu: (n, 0)),
        ),
        compiler_params=_cparams(("arbitrary",)),
        name="expert_mlp",
    )(block_e, n_used, buf, wgu, bgu.reshape(E, 1, 2 * D_FF), wdn, bdn.reshape(E, 1, D))


def _combine_kernel(x_ref, y_ref, w_ref, mod_ref, fg_ref, o_ref, *, final):
    w = w_ref[0]
    f = w[:, 0:1] * y_ref[0, 0]
    for k in range(1, TOP_K):
        f = f + w[:, k:k + 1] * y_ref[k, 0]
    xn = x_ref[0] + mod_ref[0, 5:6, :] * f
    if final:
        xn = _rms_rows(xn) * fg_ref[...]
    o_ref[0] = xn


def _combine(xy, yk, wts, mod, fg, n_x_tiles, n_q_tiles, final):
    B, SA, D = xy.shape
    SQ = n_q_tiles * ROW_TILE
    row_of = lambda b, i: jnp.where(i < n_x_tiles, b, B)
    return pl.pallas_call(
        functools.partial(_combine_kernel, final=final),
        out_shape=jax.ShapeDtypeStruct((B, SQ, D), F32),
        grid=(B, n_q_tiles),
        in_specs=[
            pl.BlockSpec((1, ROW_TILE, D), lambda b, i: (b, i, 0)),
            pl.BlockSpec((TOP_K, 1, ROW_TILE, D), lambda b, i: (0, b, i, 0)),
            pl.BlockSpec((1, ROW_TILE, TOP_K), lambda b, i: (b, i, 0)),
            pl.BlockSpec((1, 6, D), lambda b, i: (row_of(b, i), 0, 0)),
            pl.BlockSpec((1, D), lambda b, i: (0, 0)),
        ],
        out_specs=pl.BlockSpec((1, ROW_TILE, D), lambda b, i: (b, i, 0)),
        compiler_params=_cparams(("parallel", "arbitrary")),
        name="moe_combine",
    )(xy, yk, wts, mod, fg)


def _route(logits):
    T = logits.shape[0]
    TK = T * TOP_K
    NB = (TK + EXPERT_BLOCK - 1) // EXPERT_BLOCK + N_EXPERTS
    topv, topi = lax.top_k(logits, TOP_K)
    wts = jax.nn.softmax(topv, axis=-1)
    flat_e = topi.reshape(-1).astype(jnp.int32)
    order = jnp.argsort(flat_e)
    e_sorted = flat_e[order]
    sizes = jnp.bincount(flat_e, length=N_EXPERTS).astype(jnp.int32)
    padded = ((sizes + EXPERT_BLOCK - 1) // EXPERT_BLOCK) * EXPERT_BLOCK
    start = jnp.cumsum(sizes) - sizes
    pend = jnp.cumsum(padded)
    pstart = pend - padded
    dest = pstart[e_sorted] + (jnp.arange(TK, dtype=jnp.int32) - start[e_sorted])
    src_tok = jnp.zeros((NB * EXPERT_BLOCK,), jnp.int32).at[dest].set((order // TOP_K).astype(jnp.int32))
    slot = jnp.zeros((TK,), jnp.int32).at[order].set(dest)
    block_e = jnp.minimum(jnp.searchsorted(pend, jnp.arange(NB, dtype=jnp.int32) * EXPERT_BLOCK, side='right'),
                          N_EXPERTS - 1).astype(jnp.int32)
    n_used = (pend[-1] // EXPERT_BLOCK).astype(jnp.int32).reshape(1)
    return src_tok, slot.reshape(T, TOP_K), wts, block_e, n_used


def _rope_tables(S, n_ctx):
    rows = jnp.repeat(jnp.arange(S // GRID_W, dtype=F32), GRID_W)
    cols = jnp.tile(jnp.arange(GRID_W, dtype=F32), S // GRID_W)
    half = DA_HD // 2
    freqs = ROPE_BASE ** (-jnp.arange(0, half, 2, dtype=F32) / half)
    ar = rows[:, None] * freqs
    ac = cols[:, None] * freqs
    ang = jnp.concatenate([ar, ar, ac, ac], axis=-1)
    cos = jnp.concatenate([jnp.cos(ang), jnp.ones((n_ctx, DA_HD), F32)], axis=0)
    sin = jnp.concatenate([jnp.sin(ang), jnp.zeros((n_ctx, DA_HD), F32)], axis=0)
    cos = jnp.tile(cos, (1, HEAD_W // DA_HD))
    sin = jnp.tile(sin, (1, HEAD_W // DA_HD))
    lo = (jnp.arange(HEAD_W) % (DA_HD // 2)) < (DA_HD // 4)
    return cos, jnp.where(lo, -sin, 0.0), jnp.where(lo, 0.0, sin)


def kernel(x, c, ctx, c_ctx, ada_w, ada_b, norm1_g, norm2_g, w_in, da_lq1, da_lk1, da_lq2, da_lk2, da_subln_g, mla_qa_g, mla_kva_g, mla_wqb, mla_wkvb, w_br_a, w_br_b, w_out, router_w, router_b, moe_wgu, moe_bgu, moe_wdn, moe_bdn, final_g):
    B, S, D = x.shape
    CT = ctx.shape[1]
    L = ada_w.shape[0]
    SA = S + CT
    assert S % ROW_TILE == 0 and CT == ROW_TILE and S % GRID_W == 0
    n_x_tiles = S // ROW_TILE
    n_tiles = SA // ROW_TILE
    H = DA_HEADS

    xy = jnp.concatenate([x, ctx], axis=1)
    cc = jnp.concatenate([c, c_ctx[None, :], jnp.zeros((16 - B - 1, D), F32)], axis=0)
    mod_all = _ada_all(cc, ada_w, ada_b).reshape(L, 16, 6, D)
    cos, sa, sb = _rope_tables(S, CT)

    qw = H * HEAD_W
    o_cq = 3 * qw
    o_kr = o_cq + MLA_Q_RANK + MLA_KV_RANK
    o_g = o_kr + MLA_ROPE
    wm = w_in[:, :, :o_kr].astype(BF16)
    wkr = jnp.pad(w_in[:, :, o_kr:o_g], ((0, 0), (0, 0), (0, HEAD_W - MLA_ROPE))).astype(BF16)
    wg = w_in[:, :, o_g:].astype(BF16)
    wqb4 = mla_wqb.reshape(L, MLA_Q_RANK, MLA_HEADS, MLA_NOPE + MLA_ROPE)
    wqb_n = wqb4[..., :MLA_NOPE].reshape(L, MLA_Q_RANK, MLA_HEADS * MLA_NOPE)
    wqb_r = jnp.pad(wqb4[..., MLA_NOPE:], ((0, 0), (0, 0), (0, 0), (0, HEAD_W - MLA_ROPE)))
    wqb = jnp.concatenate([wqb_n, wqb_r.reshape(L, MLA_Q_RANK, MLA_HEADS * HEAD_W)], axis=-1).astype(BF16)
    wkvb4 = mla_wkvb.reshape(L, MLA_KV_RANK, MLA_HEADS, MLA_NOPE + MLA_V)
    wkvb = jnp.concatenate([wkvb4[..., :MLA_NOPE].reshape(L, MLA_KV_RANK, -1),
                            wkvb4[..., MLA_NOPE:].reshape(L, MLA_KV_RANK, -1)], axis=-1).astype(BF16)
    wa = w_br_a.astype(BF16)
    wb = w_br_b.astype(BF16)
    wo = w_out.astype(BF16)
    RW = 128
    rw = jnp.pad(router_w, ((0, 0), (0, 0), (0, RW - N_EXPERTS))).astype(BF16)
    rb = jnp.pad(router_b, ((0, 0), (0, RW - N_EXPERTS)))
    wgu = moe_wgu.astype(BF16)
    wdn = moe_wdn.astype(BF16)

    for l in range(L):
        last = l == L - 1
        n_q = n_x_tiles if last else n_tiles
        mod = mod_all[l]
        lam_init = 0.8 - 0.6 * math.exp(-0.3 * l)
        lam = (jnp.exp(jnp.sum(da_lq1[l] * da_lk1[l])) - jnp.exp(jnp.sum(da_lq2[l] * da_lk2[l])) + lam_init)
        scal = jnp.stack([lam, jnp.asarray(1.0 - lam_init, F32)]).astype(F32)

        daq, dak, dav, mq, mk, mv, gates = _pre(
            xy, mod, norm1_g[l][None], cos, sa, sb, wm[l], wkr[l], wg[l],
            mla_qa_g[l][None], mla_kva_g[l][None], wqb[l], wkvb[l], n_x_tiles)
        oda = _attention("da", daq, dak, dav, n_x_tiles, n_q, scal=scal, g=da_subln_g[l][None])
        omla = _attention("mla", mq, mk, mv, n_x_tiles, n_q)
        xy2, h2, logits = _merge(xy, oda, omla, gates, mod, norm2_g[l][None], wa[l], wb[l], wo[l],
                                 rw[l], rb[l][None], n_x_tiles, n_q)

        SQ = n_q * ROW_TILE
        T = B * SQ
        src_tok, slot, wts, block_e, n_used = _route(logits[:, :, :N_EXPERTS].reshape(T, N_EXPERTS))
        buf = h2.reshape(T, D)[src_tok]
        out_sorted = _experts(block_e, n_used, buf, wgu[l], moe_bgu[l], wdn[l], moe_bdn[l])
        yk = out_sorted[slot.T.reshape(-1)].reshape(TOP_K, B, SQ, D)
        xy = _combine(xy2, yk, wts.reshape(B, SQ, TOP_K), mod, final_g[None], n_x_tiles, n_q, last)
    return xy
```

```python
import functools
import math

import jax
import jax.numpy as jnp
from jax import lax
from jax.experimental import pallas as pl
from jax.experimental.pallas import tpu as pltpu

F32 = jnp.float32
BF16 = jnp.bfloat16

EPS = 1e-6
GRID_W = 64
ROPE_BASE = 10000.0
DA_HEADS = 4
DA_HD = 64
MLA_HEADS = 4
MLA_NOPE = 128
MLA_ROPE = 64
MLA_V = 128
MLA_Q_RANK = 384
MLA_KV_RANK = 256
N_EXPERTS = 32
TOP_K = 4
D_FF = 512
SWIGLU_LIMIT = 7.0
SWIGLU_ALPHA = 1.702
EXPERT_BLOCK = 512

LOG2E = math.log2(math.e)
HEAD_W = 128
ROW_TILE = 256
KEY_CHUNK = 512
VMEM_LIMIT = 56 * 1024 * 1024


def _cparams(sem):
    return pltpu.CompilerParams(dimension_semantics=sem, vmem_limit_bytes=VMEM_LIMIT)


def _rms_rows(x):
    return x * lax.rsqrt(jnp.mean(x * x, axis=-1, keepdims=True) + EPS)


def _ada_kernel(c_ref, w_ref, b_ref, o_ref):
    c = c_ref[...]
    s = (c * jax.nn.sigmoid(c)).astype(BF16)
    o_ref[0] = jnp.dot(s, w_ref[0].astype(BF16), preferred_element_type=F32) + b_ref[0]


def _ada_all(cc, ada_w, ada_b):
    L, D, D6 = ada_w.shape
    R = cc.shape[0]
    nj = D6 // D
    return pl.pallas_call(
        _ada_kernel,
        out_shape=jax.ShapeDtypeStruct((L, R, D6), F32),
        grid=(L, nj),
        in_specs=[
            pl.BlockSpec((R, D), lambda l, j: (0, 0)),
            pl.BlockSpec((1, D, D), lambda l, j: (l, 0, j)),
            pl.BlockSpec((1, 1, D), lambda l, j: (l, 0, j)),
        ],
        out_specs=pl.BlockSpec((1, R, D), lambda l, j: (l, 0, j)),
        compiler_params=_cparams(("arbitrary", "arbitrary")),
        name="ada_mod",
    )(cc, ada_w, ada_b.reshape(L, 1, D6))


def _rope(z, cos, sin_a, sin_b):
    return z * cos + pltpu.roll(z, 112, 1) * sin_a + pltpu.roll(z, 16, 1) * sin_b


def _pre_kernel(x_ref, mod_ref, g_ref, cos_ref, sa_ref, sb_ref, wm_ref, wkr_ref, wg_ref,
                qag_ref, kvag_ref, wqb_ref, wkvb_ref,
                daq_ref, dak_ref, dav_ref, mq_ref, mk_ref, mv_ref, gate_ref, *, da_scale, mla_scale):
    x = x_ref[0]
    shift = mod_ref[0, 0:1, :]
    scale = mod_ref[0, 1:2, :]
    h = ((_rms_rows(x) * g_ref[...]) * (1.0 + scale) + shift).astype(BF16)
    cos = cos_ref[...]
    sa = sa_ref[...]
    sb = sb_ref[...]
    nh = DA_HEADS
    qw = nh * HEAD_W

    zq = jnp.dot(h, wm_ref[:, 0:qw], preferred_element_type=F32)
    for hd in range(nh):
        z = _rope(zq[:, hd * HEAD_W:(hd + 1) * HEAD_W], cos, sa, sb) * da_scale
        daq_ref[0, hd] = z.astype(BF16)
    zk = jnp.dot(h, wm_ref[:, qw:2 * qw], preferred_element_type=F32)
    for hd in range(nh):
        z = _rope(zk[:, hd * HEAD_W:(hd + 1) * HEAD_W], cos, sa, sb)
        dak_ref[0, hd] = z.astype(BF16)
    zv = jnp.dot(h, wm_ref[:, 2 * qw:3 * qw], preferred_element_type=F32)
    for hd in range(nh):
        dav_ref[0, hd] = zv[:, hd * HEAD_W:(hd + 1) * HEAD_W].astype(BF16)

    o0 = 3 * qw
    cq = jnp.dot(h, wm_ref[:, o0:o0 + MLA_Q_RANK], preferred_element_type=F32)
    cqn = (_rms_rows(cq) * qag_ref[...]).astype(BF16)
    qm = jnp.dot(cqn, wqb_ref[...], preferred_element_type=F32)
    for hd in range(MLA_HEADS):
        mq_ref[0, hd, :, 0:HEAD_W] = (qm[:, hd * HEAD_W:(hd + 1) * HEAD_W] * mla_scale).astype(BF16)
        zr = qm[:, (MLA_HEADS + hd) * HEAD_W:(MLA_HEADS + hd + 1) * HEAD_W]
        mq_ref[0, hd, :, HEAD_W:2 * HEAD_W] = (_rope(zr, cos, sa, sb) * mla_scale).astype(BF16)

    o1 = o0 + MLA_Q_RANK
    ckv = jnp.dot(h, wm_ref[:, o1:o1 + MLA_KV_RANK], preferred_element_type=F32)
    ckvn = (_rms_rows(ckv) * kvag_ref[...]).astype(BF16)
    kv = jnp.dot(ckvn, wkvb_ref[...], preferred_element_type=F32)
    zkr = jnp.dot(h, wkr_ref[...], preferred_element_type=F32)
    kr = _rope(zkr, cos, sa, sb).astype(BF16)
    for hd in range(MLA_HEADS):
        mk_ref[0, hd, :, 0:HEAD_W] = kv[:, hd * HEAD_W:(hd + 1) * HEAD_W].astype(BF16)
        mk_ref[0, hd, :, HEAD_W:2 * HEAD_W] = kr
        mv_ref[0, hd] = kv[:, (MLA_HEADS + hd) * HEAD_W:(MLA_HEADS + hd + 1) * HEAD_W].astype(BF16)

    gz = jnp.dot(h, wg_ref[...], preferred_element_type=F32)
    gate_ref[0] = jax.nn.sigmoid(gz).astype(BF16)


def _pre(xy, mod, g1, cos, sa, sb, wm, wkr, wg, qag, kvag, wqb, wkvb, n_x_tiles):
    B, SA, D = xy.shape
    nt = SA // ROW_TILE
    H = DA_HEADS
    const2 = lambda b, i: (0, 0)
    head_spec = lambda w: pl.BlockSpec((1, H, ROW_TILE, w), lambda b, i: (b, 0, i, 0))
    row_of = lambda b, i: jnp.where(i < n_x_tiles, b, B)
    kern = functools.partial(_pre_kernel, da_scale=LOG2E / math.sqrt(DA_HD),
                             mla_scale=LOG2E / math.sqrt(MLA_NOPE + MLA_ROPE))
    return pl.pallas_call(
        kern,
        out_shape=(
            jax.ShapeDtypeStruct((B, H, SA, HEAD_W), BF16),
            jax.ShapeDtypeStruct((B, H, SA, HEAD_W), BF16),
            jax.ShapeDtypeStruct((B, H, SA, HEAD_W), BF16),
            jax.ShapeDtypeStruct((B, H, SA, 2 * HEAD_W), BF16),
            jax.ShapeDtypeStruct((B, H, SA, 2 * HEAD_W), BF16),
            jax.ShapeDtypeStruct((B, H, SA, HEAD_W), BF16),
            jax.ShapeDtypeStruct((B, SA, 2 * D), BF16),
        ),
        grid=(B, nt),
        in_specs=[
            pl.BlockSpec((1, ROW_TILE, D), lambda b, i: (b, i, 0)),
            pl.BlockSpec((1, 6, D), lambda b, i: (row_of(b, i), 0, 0)),
            pl.BlockSpec((1, D), const2),
            pl.BlockSpec((ROW_TILE, HEAD_W), lambda b, i: (i, 0)),
            pl.BlockSpec((ROW_TILE, HEAD_W), lambda b, i: (i, 0)),
            pl.BlockSpec((ROW_TILE, HEAD_W), lambda b, i: (i, 0)),
            pl.BlockSpec(wm.shape, const2),
            pl.BlockSpec(wkr.shape, const2),
            pl.BlockSpec(wg.shape, const2),
            pl.BlockSpec(qag.shape, const2),
            pl.BlockSpec(kvag.shape, const2),
            pl.BlockSpec(wqb.shape, const2),
            pl.BlockSpec(wkvb.shape, const2),
        ],
        out_specs=(
            head_spec(HEAD_W), head_spec(HEAD_W), head_spec(HEAD_W),
            head_spec(2 * HEAD_W), head_spec(2 * HEAD_W), head_spec(HEAD_W),
            pl.BlockSpec((1, ROW_TILE, 2 * D), lambda b, i: (b, i, 0)),
        ),
        compiler_params=_cparams(("parallel", "arbitrary")),
        name="pre_mixer",
    )(xy, mod, g1, cos, sa, sb, wm, wkr, wg, qag, kvag, wqb, wkvb)


def _key_chunks(start, stop):
    out = []
    while start < stop:
        size = min(KEY_CHUNK, stop - start)
        out.append((start, size))
        start += size
    return out


def _fill_value_ext(v_ref, vx_ref):
    vx_ref[:, 0:HEAD_W] = v_ref[0, 0]
    vx_ref[:, HEAD_W:2 * HEAD_W] = jnp.ones((v_ref.shape[2], HEAD_W), BF16)


def _flash(qq, k_ref, vx_ref, chunks):
    m = None
    acc = None
    for (st, sz) in chunks:
        s = lax.dot_general(qq, k_ref[0, 0, st:st + sz, :], (((1,), (1,)), ((), ())),
                            preferred_element_type=F32)
        mc = jnp.max(s, axis=-1, keepdims=True)
        m_new = mc if m is None else jnp.maximum(m, mc)
        e = jnp.exp2(s - m_new)
        pc = jnp.dot(e.astype(BF16), vx_ref[st:st + sz, :], preferred_element_type=F32)
        acc = pc if m is None else jnp.exp2(m - m_new) * acc + pc
        m = m_new
    return acc[:, :HEAD_W] / acc[:, HEAD_W:HEAD_W + 1]


def _da_kernel(sc_ref, q_ref, k_ref, v_ref, g_ref, o_ref, vx_ref, *, n_x_tiles, n_x_keys, n_keys):
    i = pl.program_id(2)
    tq = q_ref.shape[2]

    @pl.when(i == 0)
    def _():
        _fill_value_ext(v_ref, vx_ref)

    def run(chunks):
        q = q_ref[0, 0]
        lane = lax.broadcasted_iota(jnp.int32, q.shape, 1)
        zero = jnp.zeros_like(q)
        qq = jnp.concatenate([jnp.where(lane < DA_HD, q, zero), jnp.where(lane >= DA_HD, q, zero)], axis=0)
        o = _flash(qq, k_ref, vx_ref, chunks)
        od = o[:tq] - sc_ref[0] * o[tq:]
        od = (_rms_rows(od) * g_ref[...]) * sc_ref[1]
        o_ref[0] = od.astype(BF16)

    @pl.when(i < n_x_tiles)
    def _():
        run(_key_chunks(0, n_keys))

    @pl.when(i >= n_x_tiles)
    def _():
        run(_key_chunks(n_x_keys, n_keys))


def _mla_kernel(q_ref, k_ref, v_ref, o_ref, vx_ref, *, n_x_tiles, n_x_keys, n_keys):
    i = pl.program_id(2)

    @pl.when(i == 0)
    def _():
        _fill_value_ext(v_ref, vx_ref)

    def run(chunks):
        o_ref[0] = _flash(q_ref[0, 0], k_ref, vx_ref, chunks).astype(BF16)

    @pl.when(i < n_x_tiles)
    def _():
        run(_key_chunks(0, n_keys))

    @pl.when(i >= n_x_tiles)
    def _():
        run(_key_chunks(n_x_keys, n_keys))


def _attention(kind, q, k, v, n_x_tiles, n_q_tiles, scal=None, g=None):
    B, H, SA, QW = q.shape
    tq = ROW_TILE
    n_x_keys = n_x_tiles * ROW_TILE
    kw = dict(n_x_tiles=n_x_tiles, n_x_keys=n_x_keys, n_keys=SA)
    q_spec = pl.BlockSpec((1, 1, tq, QW), lambda b, h, i: (b, h, i, 0))
    k_spec = pl.BlockSpec((1, 1, SA, QW), lambda b, h, i: (b, h, 0, 0))
    v_spec = pl.BlockSpec((1, 1, SA, HEAD_W), lambda b, h, i: (b, h, 0, 0))
    o_spec = pl.BlockSpec((1, tq, HEAD_W), lambda b, h, i: (b, i, h))
    out_shape = jax.ShapeDtypeStruct((B, n_q_tiles * tq, H * HEAD_W), BF16)
    sem = ("parallel", "parallel", "arbitrary")
    if kind == "da":
        return pl.pallas_call(
            functools.partial(_da_kernel, **kw),
            out_shape=out_shape,
            grid=(B, H, n_q_tiles),
            in_specs=[pl.BlockSpec(memory_space=pltpu.SMEM), q_spec, k_spec, v_spec,
                      pl.BlockSpec((1, HEAD_W), lambda b, h, i: (0, 0))],
            out_specs=o_spec,
            scratch_shapes=[pltpu.VMEM((SA, 2 * HEAD_W), BF16)],
            compiler_params=_cparams(sem),
            name="diff_attention",
        )(scal, q, k, v, g)
    return pl.pallas_call(
        functools.partial(_mla_kernel, **kw),
        out_shape=out_shape,
        grid=(B, H, n_q_tiles),
        in_specs=[q_spec, k_spec, v_spec],
        out_specs=o_spec,
        scratch_shapes=[pltpu.VMEM((SA, 2 * HEAD_W), BF16)],
        compiler_params=_cparams(sem),
        name="mla_attention",
    )(q, k, v)


ROUTE_W = 128
NEG_BIG = -3.0e38


def _merge_kernel(x_ref, oda_ref, omla_ref, gate_ref, mod_ref, g2_ref, wa_ref, wb_ref, wo_ref,
                  rw_ref, rb_ref, xo_ref, h2_ref, route_ref, cnt_ref, run_ref):
    D = x_ref.shape[2]
    tm = x_ref.shape[1]
    a = jnp.dot(oda_ref[0], wa_ref[...], preferred_element_type=F32)
    b = jnp.dot(omla_ref[0], wb_ref[...], preferred_element_type=F32)
    ga = gate_ref[0, :, 0:D].astype(F32)
    gb = gate_ref[0, :, D:2 * D].astype(F32)
    m = (ga * a + gb * b).astype(BF16)
    mix = jnp.dot(m, wo_ref[...], preferred_element_type=F32)
    xn = x_ref[0] + mod_ref[0, 2:3, :] * mix
    xo_ref[0] = xn
    h2 = (_rms_rows(xn) * g2_ref[...]) * (1.0 + mod_ref[0, 4:5, :]) + mod_ref[0, 3:4, :]
    h2_ref[0] = h2
    lg = jnp.dot(h2.astype(BF16), rw_ref[...], preferred_element_type=F32) + rb_ref[...]

    @pl.when((pl.program_id(0) == 0) & (pl.program_id(1) == 0))
    def _():
        run_ref[...] = jnp.zeros_like(run_ref)

    lane = lax.broadcasted_iota(jnp.int32, lg.shape, 1)
    lanef = lane.astype(F32)
    earlier = (lax.broadcasted_iota(jnp.int32, (tm, tm), 0) > lax.broadcasted_iota(jnp.int32, (tm, tm), 1))
    earlier = jnp.where(earlier, 1.0, 0.0).astype(BF16)
    run = run_ref[0:1, :]
    route = jnp.zeros(lg.shape, F32)
    vals = []
    for r in range(TOP_K):
        mx = jnp.max(lg, axis=-1, keepdims=True)
        idx = jnp.min(jnp.where(lg == mx, lanef, float(ROUTE_W)), axis=-1, keepdims=True)
        hit = lanef == idx
        hitf = jnp.where(hit, 1.0, 0.0)
        before = jnp.dot(earlier, hitf.astype(BF16), preferred_element_type=F32)
        rank = jnp.sum(jnp.where(hit, run + before, 0.0), axis=-1, keepdims=True)
        run = run + jnp.sum(hitf, axis=0, keepdims=True)
        route = jnp.where(lane == r, idx, route)
        route = jnp.where(lane == TOP_K + r, rank, route)
        vals.append(mx)
        lg = jnp.where(hit, NEG_BIG, lg)
    ex = [jnp.exp(v - vals[0]) for v in vals]
    den = ex[0] + ex[1] + ex[2] + ex[3]
    for r in range(TOP_K):
        route = jnp.where(lane == 2 * TOP_K + r, ex[r] / den, route)
    route_ref[0] = route
    run_ref[...] = jnp.broadcast_to(run, run_ref.shape)
    cnt_ref[...] = jnp.broadcast_to(run, cnt_ref.shape)


def _merge(xy, oda, omla, gates, mod, g2, wa, wb, wo, rw, rb, n_x_tiles, n_q_tiles):
    B, SA, D = xy.shape
    const2 = lambda b, i: (0, 0)
    row_of = lambda b, i: jnp.where(i < n_x_tiles, b, B)
    row_spec = lambda w: pl.BlockSpec((1, ROW_TILE, w), lambda b, i: (b, i, 0))
    SQ = n_q_tiles * ROW_TILE
    return pl.pallas_call(
        _merge_kernel,
        out_shape=(
            jax.ShapeDtypeStruct((B, SQ, D), F32),
            jax.ShapeDtypeStruct((B, SQ, D), F32),
            jax.ShapeDtypeStruct((B, SQ, ROUTE_W), F32),
            jax.ShapeDtypeStruct((8, ROUTE_W), F32),
        ),
        grid=(B, n_q_tiles),
        in_specs=[
            row_spec(D), row_spec(oda.shape[2]), row_spec(omla.shape[2]), row_spec(2 * D),
            pl.BlockSpec((1, 6, D), lambda b, i: (row_of(b, i), 0, 0)),
            pl.BlockSpec((1, D), const2),
            pl.BlockSpec(wa.shape, const2), pl.BlockSpec(wb.shape, const2), pl.BlockSpec(wo.shape, const2),
            pl.BlockSpec(rw.shape, const2), pl.BlockSpec(rb.shape, const2),
        ],
        out_specs=(row_spec(D), row_spec(D), row_spec(ROUTE_W), pl.BlockSpec((8, ROUTE_W), const2)),
        scratch_shapes=[pltpu.VMEM((8, ROUTE_W), F32)],
        compiler_params=_cparams(("arbitrary", "arbitrary")),
        name="merge_post",
    )(xy, oda, omla, gates, mod, g2, wa, wb, wo, rw, rb)


def _row_copy(src_ref, src_row, dst_ref, dst_row, sem):
    return pltpu.make_async_copy(src_ref.at[pl.ds(src_row, 1)], dst_ref.at[pl.ds(dst_row, 1)], sem)


def _dispatch_kernel(slot_ref, h_ref, buf_in_ref, buf_ref, sem):
    del buf_in_ref
    tm = h_ref.shape[0]

    def issue(r, carry):
        for k in range(TOP_K):
            _row_copy(h_ref, r, buf_ref, slot_ref[0, 0, r * TOP_K + k], sem).start()
        return carry

    lax.fori_loop(0, tm, issue, 0)

    def drain(r, carry):
        for k in range(TOP_K):
            _row_copy(h_ref, 0, buf_ref, 0, sem).wait()
        return carry

    lax.fori_loop(0, tm, drain, 0)


def _dispatch(slot3, h2, buf):
    T, D = h2.shape
    nt = T // ROW_TILE
    return pl.pallas_call(
        _dispatch_kernel,
        out_shape=jax.ShapeDtypeStruct(buf.shape, buf.dtype),
        grid=(nt,),
        in_specs=[
            pl.BlockSpec((1, 1, ROW_TILE * TOP_K), lambda t: (t, 0, 0), memory_space=pltpu.SMEM),
            pl.BlockSpec((ROW_TILE, D), lambda t: (t, 0)),
            pl.BlockSpec(memory_space=pl.ANY),
        ],
        out_specs=pl.BlockSpec(memory_space=pl.ANY),
        scratch_shapes=[pltpu.SemaphoreType.DMA(())],
        input_output_aliases={2: 0},
        compiler_params=_cparams(("arbitrary",)),
        name="moe_dispatch",
    )(slot3, h2, buf)


def _expert_kernel(be_ref, nu_ref, x_ref, wgu_ref, bgu_ref, wdn_ref, bdn_ref, o_ref):
    n = pl.program_id(0)

    @pl.when(n < nu_ref[0])
    def _():
        h = jnp.dot(x_ref[...].astype(BF16), wgu_ref[0], preferred_element_type=F32) + bgu_ref[0]
        gate = jnp.minimum(h[:, :D_FF], SWIGLU_LIMIT)
        lin = jnp.clip(h[:, D_FF:], -SWIGLU_LIMIT, SWIGLU_LIMIT)
        act = (lin + 1.0) * gate * jax.nn.sigmoid(SWIGLU_ALPHA * gate)
        o_ref[...] = jnp.dot(act.astype(BF16), wdn_ref[0], preferred_element_type=F32) + bdn_ref[0]

    @pl.when(n >= nu_ref[0])
    def _():
        o_ref[...] = jnp.zeros_like(o_ref)


def _experts(block_e, n_used, buf, wgu, bgu, wdn, bdn):
    R, D = buf.shape
    NB = R // EXPERT_BLOCK
    E = wgu.shape[0]
    return pl.pallas_call(
        _expert_kernel,
        out_shape=jax.ShapeDtypeStruct((R, D), F32),
        grid_spec=pltpu.PrefetchScalarGridSpec(
            num_scalar_prefetch=2,
            grid=(NB,),
            in_specs=[
                pl.BlockSpec((EXPERT_BLOCK, D), lambda n, be, nu: (jnp.minimum(n, nu[0] - 1), 0)),
                pl.BlockSpec((1, D, 2 * D_FF), lambda n, be, nu: (be[n], 0, 0)),
                pl.BlockSpec((1, 1, 2 * D_FF), lambda n, be, nu: (be[n], 0, 0)),
                pl.BlockSpec((1, D_FF, D), lambda n, be, nu: (be[n], 0, 0)),
                pl.BlockSpec((1, 1, D), lambda n, be, nu: (be[n], 0, 0)),
            ],
            out_specs=pl.BlockSpec((EXPERT_BLOCK, D), lambda n, be, nu: (n, 0)),
        ),
        compiler_params=_cparams(("arbitrary",)),
        name="expert_mlp",
    )(block_e, n_used, buf, wgu, bgu.reshape(E, 1, 2 * D_FF), wdn, bdn.reshape(E, 1, D))


def _combine_kernel(slot_ref, nslot_ref, x_ref, route_ref, mod_ref, fg_ref, y_hbm, o_ref, ybuf, sems, *, final):
    t = pl.program_id(0)
    nt = pl.num_programs(0)
    tm = x_ref.shape[0]
    cur = lax.rem(t, 2)

    def gather(s_ref, half):
        def issue(r, carry):
            for k in range(TOP_K):
                _row_copy(y_hbm, s_ref[0, 0, r * TOP_K + k], ybuf.at[half, k], r, sems.at[half]).start()
            return carry
        lax.fori_loop(0, tm, issue, 0)

    @pl.when(t == 0)
    def _():
        gather(slot_ref, 0)

    @pl.when(t + 1 < nt)
    def _():
        gather(nslot_ref, 1 - cur)

    def drain(r, carry):
        for k in range(TOP_K):
            _row_copy(y_hbm, 0, ybuf.at[cur, k], 0, sems.at[cur]).wait()
        return carry

    lax.fori_loop(0, tm, drain, 0)

    w = route_ref[:, 2 * TOP_K:3 * TOP_K]
    f = w[:, 0:1] * ybuf[cur, 0]
    for k in range(1, TOP_K):
        f = f + w[:, k:k + 1] * ybuf[cur, k]
    xn = x_ref[...] + mod_ref[0, 5:6, :] * f
    if final:
        xn = _rms_rows(xn) * fg_ref[...]
    o_ref[...] = xn


def _combine(x2, y_sorted, slot3, route, mod, fg, n_x_tiles, n_q_tiles, n_batch, final):
    T, D = x2.shape
    nt = T // ROW_TILE
    row_of = lambda t: jnp.where(t % n_q_tiles < n_x_tiles, t // n_q_tiles, n_batch)
    slot_block = (1, 1, ROW_TILE * TOP_K)
    return pl.pallas_call(
        functools.partial(_combine_kernel, final=final),
        out_shape=jax.ShapeDtypeStruct((T, D), F32),
        grid=(nt,),
        in_specs=[
            pl.BlockSpec(slot_block, lambda t: (t, 0, 0), memory_space=pltpu.SMEM),
            pl.BlockSpec(slot_block, lambda t: (jnp.minimum(t + 1, nt - 1), 0, 0), memory_space=pltpu.SMEM),
            pl.BlockSpec((ROW_TILE, D), lambda t: (t, 0)),
            pl.BlockSpec((ROW_TILE, ROUTE_W), lambda t: (t, 0)),
            pl.BlockSpec((1, 6, D), lambda t: (row_of(t), 0, 0)),
            pl.BlockSpec((1, D), lambda t: (0, 0)),
            pl.BlockSpec(memory_space=pl.ANY),
        ],
        out_specs=pl.BlockSpec((ROW_TILE, D), lambda t: (t, 0)),
        scratch_shapes=[pltpu.VMEM((2, TOP_K, ROW_TILE, D), F32), pltpu.SemaphoreType.DMA((2,))],
        compiler_params=_cparams(("arbitrary",)),
        name="moe_combine",
    )(slot3, slot3, x2, route, mod, fg, y_sorted)


def _block_layout(counts, n_blocks):
    sizes = counts[0, :N_EXPERTS].astype(jnp.int32)
    padded = ((sizes + EXPERT_BLOCK - 1) // EXPERT_BLOCK) * EXPERT_BLOCK
    pend = jnp.cumsum(padded)
    pstart = pend - padded
    first_row = jnp.arange(n_blocks, dtype=jnp.int32) * EXPERT_BLOCK
    block_e = jnp.minimum(jnp.sum(pend[None, :] <= first_row[:, None], axis=1), N_EXPERTS - 1).astype(jnp.int32)
    n_used = (pend[-1] // EXPERT_BLOCK).astype(jnp.int32).reshape(1)
    return pstart, block_e, n_used


def _rope_tables(S, n_ctx):
    rows = jnp.repeat(jnp.arange(S // GRID_W, dtype=F32), GRID_W)
    cols = jnp.tile(jnp.arange(GRID_W, dtype=F32), S // GRID_W)
    half = DA_HD // 2
    freqs = ROPE_BASE ** (-jnp.arange(0, half, 2, dtype=F32) / half)
    ar = rows[:, None] * freqs
    ac = cols[:, None] * freqs
    ang = jnp.concatenate([ar, ar, ac, ac], axis=-1)
    cos = jnp.concatenate([jnp.cos(ang), jnp.ones((n_ctx, DA_HD), F32)], axis=0)
    sin = jnp.concatenate([jnp.sin(ang), jnp.zeros((n_ctx, DA_HD), F32)], axis=0)
    cos = jnp.tile(cos, (1, HEAD_W // DA_HD))
    sin = jnp.tile(sin, (1, HEAD_W // DA_HD))
    lo = (jnp.arange(HEAD_W) % (DA_HD // 2)) < (DA_HD // 4)
    return cos, jnp.where(lo, -sin, 0.0), jnp.where(lo, 0.0, sin)


def kernel(x, c, ctx, c_ctx, ada_w, ada_b, norm1_g, norm2_g, w_in, da_lq1, da_lk1, da_lq2, da_lk2, da_subln_g, mla_qa_g, mla_kva_g, mla_wqb, mla_wkvb, w_br_a, w_br_b, w_out, router_w, router_b, moe_wgu, moe_bgu, moe_wdn, moe_bdn, final_g):
    B, S, D = x.shape
    CT = ctx.shape[1]
    L = ada_w.shape[0]
    SA = S + CT
    assert S % ROW_TILE == 0 and CT == ROW_TILE and S % GRID_W == 0
    n_x_tiles = S // ROW_TILE
    n_tiles = SA // ROW_TILE
    H = DA_HEADS

    xy = jnp.concatenate([x, ctx], axis=1)
    cc = jnp.concatenate([c, c_ctx[None, :], jnp.zeros((16 - B - 1, D), F32)], axis=0)
    mod_all = _ada_all(cc, ada_w, ada_b).reshape(L, 16, 6, D)
    cos, sa, sb = _rope_tables(S, CT)

    qw = H * HEAD_W
    o_cq = 3 * qw
    o_kr = o_cq + MLA_Q_RANK + MLA_KV_RANK
    o_g = o_kr + MLA_ROPE
    wm = w_in[:, :, :o_kr].astype(BF16)
    wkr = jnp.pad(w_in[:, :, o_kr:o_g], ((0, 0), (0, 0), (0, HEAD_W - MLA_ROPE))).astype(BF16)
    wg = w_in[:, :, o_g:].astype(BF16)
    wqb4 = mla_wqb.reshape(L, MLA_Q_RANK, MLA_HEADS, MLA_NOPE + MLA_ROPE)
    wqb_n = wqb4[..., :MLA_NOPE].reshape(L, MLA_Q_RANK, MLA_HEADS * MLA_NOPE)
    wqb_r = jnp.pad(wqb4[..., MLA_NOPE:], ((0, 0), (0, 0), (0, 0), (0, HEAD_W - MLA_ROPE)))
    wqb = jnp.concatenate([wqb_n, wqb_r.reshape(L, MLA_Q_RANK, MLA_HEADS * HEAD_W)], axis=-1).astype(BF16)
    wkvb4 = mla_wkvb.reshape(L, MLA_KV_RANK, MLA_HEADS, MLA_NOPE + MLA_V)
    wkvb = jnp.concatenate([wkvb4[..., :MLA_NOPE].reshape(L, MLA_KV_RANK, -1),
                            wkvb4[..., MLA_NOPE:].reshape(L, MLA_KV_RANK, -1)], axis=-1).astype(BF16)
    wa = w_br_a.astype(BF16)
    wb = w_br_b.astype(BF16)
    wo = w_out.astype(BF16)
    RW = ROUTE_W
    rw = jnp.pad(router_w, ((0, 0), (0, 0), (0, RW - N_EXPERTS))).astype(BF16)
    rb = jnp.pad(router_b, ((0, 0), (0, RW - N_EXPERTS)), constant_values=NEG_BIG)
    wgu = moe_wgu.astype(BF16)
    wdn = moe_wdn.astype(BF16)
    NB = (B * SA * TOP_K + EXPERT_BLOCK - 1) // EXPERT_BLOCK + N_EXPERTS
    buf = jnp.zeros((NB * EXPERT_BLOCK, D), F32)

    for l in range(L):
        last = l == L - 1
        n_q = n_x_tiles if last else n_tiles
        mod = mod_all[l]
        lam_init = 0.8 - 0.6 * math.exp(-0.3 * l)
        lam = (jnp.exp(jnp.sum(da_lq1[l] * da_lk1[l])) - jnp.exp(jnp.sum(da_lq2[l] * da_lk2[l])) + lam_init)
        scal = jnp.stack([lam, jnp.asarray(1.0 - lam_init, F32)]).astype(F32)

        daq, dak, dav, mq, mk, mv, gates = _pre(
            xy, mod, norm1_g[l][None], cos, sa, sb, wm[l], wkr[l], wg[l],
            mla_qa_g[l][None], mla_kva_g[l][None], wqb[l], wkvb[l], n_x_tiles)
        oda = _attention("da", daq, dak, dav, n_x_tiles, n_q, scal=scal, g=da_subln_g[l][None])
        omla = _attention("mla", mq, mk, mv, n_x_tiles, n_q)
        xy2, h2, route, counts = _merge(xy, oda, omla, gates, mod, norm2_g[l][None], wa[l], wb[l], wo[l],
                                 rw[l], rb[l][None], n_x_tiles, n_q)

        SQ = n_q * ROW_TILE
        T = B * SQ
        pstart, block_e, n_used = _block_layout(counts, NB)
        route2 = route.reshape(T, ROUTE_W)
        e_idx = route2[:, 0:TOP_K].astype(jnp.int32)
        rank = route2[:, TOP_K:2 * TOP_K].astype(jnp.int32)
        base = jnp.sum(jnp.where(e_idx[:, :, None] == jnp.arange(N_EXPERTS, dtype=jnp.int32), pstart, 0), axis=-1)
        slot3 = (base + rank).reshape(T // ROW_TILE, 1, ROW_TILE * TOP_K)
        buf = _dispatch(slot3, h2.reshape(T, D), buf)
        out_sorted = _experts(block_e, n_used, buf, wgu[l], moe_bgu[l], wdn[l], moe_bdn[l])
        xy = _combine(xy2.reshape(T, D), out_sorted, slot3, route2, mod, final_g[None],
                      n_x_tiles, n_q, B, last).reshape(B, SQ, D)
    return xy
```

```python
import functools
import math

import jax
import jax.numpy as jnp
from jax import lax
from jax.experimental import pallas as pl
from jax.experimental.pallas import tpu as pltpu

F32 = jnp.float32
BF16 = jnp.bfloat16

EPS = 1e-6
GRID_W = 64
ROPE_BASE = 10000.0
DA_HEADS = 4
DA_HD = 64
MLA_HEADS = 4
MLA_NOPE = 128
MLA_ROPE = 64
MLA_V = 128
MLA_Q_RANK = 384
MLA_KV_RANK = 256
N_EXPERTS = 32
TOP_K = 4
D_FF = 512
SWIGLU_LIMIT = 7.0
SWIGLU_ALPHA = 1.702
EXPERT_BLOCK = 512

LOG2E = math.log2(math.e)
HEAD_W = 128
ROW_TILE = 256
ATT_TILE = 512
KEY_CHUNK = 512
VMEM_LIMIT = 56 * 1024 * 1024


def _cparams(sem):
    return pltpu.CompilerParams(dimension_semantics=sem, vmem_limit_bytes=VMEM_LIMIT)


def _rms_rows(x):
    return x * lax.rsqrt(jnp.mean(x * x, axis=-1, keepdims=True) + EPS)


def _ada_kernel(c_ref, w_ref, b_ref, o_ref):
    c = c_ref[...]
    s = (c * jax.nn.sigmoid(c)).astype(BF16)
    o_ref[0] = jnp.dot(s, w_ref[0].astype(BF16), preferred_element_type=F32) + b_ref[0]


def _ada_all(cc, ada_w, ada_b):
    L, D, D6 = ada_w.shape
    R = cc.shape[0]
    nj = D6 // D
    return pl.pallas_call(
        _ada_kernel,
        out_shape=jax.ShapeDtypeStruct((L, R, D6), F32),
        grid=(L, nj),
        in_specs=[
            pl.BlockSpec((R, D), lambda l, j: (0, 0)),
            pl.BlockSpec((1, D, D), lambda l, j: (l, 0, j)),
            pl.BlockSpec((1, 1, D), lambda l, j: (l, 0, j)),
        ],
        out_specs=pl.BlockSpec((1, R, D), lambda l, j: (l, 0, j)),
        compiler_params=_cparams(("arbitrary", "arbitrary")),
        name="ada_mod",
    )(cc, ada_w, ada_b.reshape(L, 1, D6))


def _rope(z, cos, sin_a, sin_b):
    return z * cos + pltpu.roll(z, 112, 1) * sin_a + pltpu.roll(z, 16, 1) * sin_b


def _pre_kernel(x_ref, mod_ref, g_ref, cos_ref, sa_ref, sb_ref, wm_ref, wkr_ref, wg_ref,
                qag_ref, kvag_ref, wqb_ref, wkvb_ref,
                daq_ref, dak_ref, dav_ref, mq_ref, mk_ref, mv_ref, gate_ref, *, da_scale, mla_scale):
    x = x_ref[0]
    shift = mod_ref[0, 0:1, :]
    scale = mod_ref[0, 1:2, :]
    h = ((_rms_rows(x) * g_ref[...]) * (1.0 + scale) + shift).astype(BF16)
    cos = cos_ref[...]
    sa = sa_ref[...]
    sb = sb_ref[...]
    nh = DA_HEADS
    qw = nh * HEAD_W

    zq = jnp.dot(h, wm_ref[:, 0:qw], preferred_element_type=F32)
    for hd in range(nh):
        z = _rope(zq[:, hd * HEAD_W:(hd + 1) * HEAD_W], cos, sa, sb) * da_scale
        daq_ref[0, hd] = z.astype(BF16)
    zk = jnp.dot(h, wm_ref[:, qw:2 * qw], preferred_element_type=F32)
    for hd in range(nh):
        z = _rope(zk[:, hd * HEAD_W:(hd + 1) * HEAD_W], cos, sa, sb)
        dak_ref[0, hd] = z.astype(BF16)
    zv = jnp.dot(h, wm_ref[:, 2 * qw:3 * qw], preferred_element_type=F32)
    for hd in range(nh):
        dav_ref[0, hd] = zv[:, hd * HEAD_W:(hd + 1) * HEAD_W].astype(BF16)

    o0 = 3 * qw
    cq = jnp.dot(h, wm_ref[:, o0:o0 + MLA_Q_RANK], preferred_element_type=F32)
    cqn = (_rms_rows(cq) * qag_ref[...]).astype(BF16)
    qm = jnp.dot(cqn, wqb_ref[...], preferred_element_type=F32)
    for hd in range(MLA_HEADS):
        mq_ref[0, hd, :, 0:HEAD_W] = (qm[:, hd * HEAD_W:(hd + 1) * HEAD_W] * mla_scale).astype(BF16)
        zr = qm[:, (MLA_HEADS + hd) * HEAD_W:(MLA_HEADS + hd + 1) * HEAD_W]
        mq_ref[0, hd, :, HEAD_W:2 * HEAD_W] = (_rope(zr, cos, sa, sb) * mla_scale).astype(BF16)

    o1 = o0 + MLA_Q_RANK
    ckv = jnp.dot(h, wm_ref[:, o1:o1 + MLA_KV_RANK], preferred_element_type=F32)
    ckvn = (_rms_rows(ckv) * kvag_ref[...]).astype(BF16)
    kv = jnp.dot(ckvn, wkvb_ref[...], preferred_element_type=F32)
    zkr = jnp.dot(h, wkr_ref[...], preferred_element_type=F32)
    kr = _rope(zkr, cos, sa, sb).astype(BF16)
    for hd in range(MLA_HEADS):
        mk_ref[0, hd, :, 0:HEAD_W] = kv[:, hd * HEAD_W:(hd + 1) * HEAD_W].astype(BF16)
        mk_ref[0, hd, :, HEAD_W:2 * HEAD_W] = kr
        mv_ref[0, hd] = kv[:, (MLA_HEADS + hd) * HEAD_W:(MLA_HEADS + hd + 1) * HEAD_W].astype(BF16)

    gz = jnp.dot(h, wg_ref[...], preferred_element_type=F32)
    gate_ref[0] = jax.nn.sigmoid(gz).astype(BF16)


def _pre(xy, mod, g1, cos, sa, sb, wm, wkr, wg, qag, kvag, wqb, wkvb, n_x_tiles):
    B, SA, D = xy.shape
    nt = SA // ROW_TILE
    H = DA_HEADS
    const2 = lambda b, i: (0, 0)
    head_spec = lambda w: pl.BlockSpec((1, H, ROW_TILE, w), lambda b, i: (b, 0, i, 0))
    row_of = lambda b, i: jnp.where(i < n_x_tiles, b, B)
    kern = functools.partial(_pre_kernel, da_scale=LOG2E / math.sqrt(DA_HD),
                             mla_scale=LOG2E / math.sqrt(MLA_NOPE + MLA_ROPE))
    return pl.pallas_call(
        kern,
        out_shape=(
            jax.ShapeDtypeStruct((B, H, SA, HEAD_W), BF16),
            jax.ShapeDtypeStruct((B, H, SA, HEAD_W), BF16),
            jax.ShapeDtypeStruct((B, H, SA, HEAD_W), BF16),
            jax.ShapeDtypeStruct((B, H, SA, 2 * HEAD_W), BF16),
            jax.ShapeDtypeStruct((B, H, SA, 2 * HEAD_W), BF16),
            jax.ShapeDtypeStruct((B, H, SA, HEAD_W), BF16),
            jax.ShapeDtypeStruct((B, SA, 2 * D), BF16),
        ),
        grid=(B, nt),
        in_specs=[
            pl.BlockSpec((1, ROW_TILE, D), lambda b, i: (b, i, 0)),
            pl.BlockSpec((1, 6, D), lambda b, i: (row_of(b, i), 0, 0)),
            pl.BlockSpec((1, D), const2),
            pl.BlockSpec((ROW_TILE, HEAD_W), lambda b, i: (i, 0)),
            pl.BlockSpec((ROW_TILE, HEAD_W), lambda b, i: (i, 0)),
            pl.BlockSpec((ROW_TILE, HEAD_W), lambda b, i: (i, 0)),
            pl.BlockSpec(wm.shape, const2),
            pl.BlockSpec(wkr.shape, const2),
            pl.BlockSpec(wg.shape, const2),
            pl.BlockSpec(qag.shape, const2),
            pl.BlockSpec(kvag.shape, const2),
            pl.BlockSpec(wqb.shape, const2),
            pl.BlockSpec(wkvb.shape, const2),
        ],
        out_specs=(
            head_spec(HEAD_W), head_spec(HEAD_W), head_spec(HEAD_W),
            head_spec(2 * HEAD_W), head_spec(2 * HEAD_W), head_spec(HEAD_W),
            pl.BlockSpec((1, ROW_TILE, 2 * D), lambda b, i: (b, i, 0)),
        ),
        compiler_params=_cparams(("parallel", "arbitrary")),
        name="pre_mixer",
    )(xy, mod, g1, cos, sa, sb, wm, wkr, wg, qag, kvag, wqb, wkvb)


def _key_chunks(start, stop):
    out = []
    while start < stop:
        size = min(KEY_CHUNK, stop - start)
        out.append((start, size))
        start += size
    return out


def _fill_value_ext(v_ref, vx_ref):
    vx_ref[:, 0:HEAD_W] = v_ref[0, 0]
    vx_ref[:, HEAD_W:2 * HEAD_W] = jnp.ones((v_ref.shape[2], HEAD_W), BF16)


def _flash(qq, k_ref, vx_ref, chunks):
    m = None
    acc = None
    for (st, sz) in chunks:
        s = lax.dot_general(qq, k_ref[0, 0, st:st + sz, :], (((1,), (1,)), ((), ())),
                            preferred_element_type=F32)
        mc = jnp.max(s, axis=-1, keepdims=True)
        m_new = mc if m is None else jnp.maximum(m, mc)
        e = jnp.exp2(s - m_new)
        pc = jnp.dot(e.astype(BF16), vx_ref[st:st + sz, :], preferred_element_type=F32)
        acc = pc if m is None else jnp.exp2(m - m_new) * acc + pc
        m = m_new
    return acc[:, :HEAD_W] / acc[:, HEAD_W:HEAD_W + 1]


def _att_kernel(*refs, kind, n_x_tiles, n_x_keys, n_keys, with_ctx):
    it = iter(refs)
    sc_ref = next(it) if kind == "da" else None
    q_ref = next(it)
    qc_ref = next(it) if with_ctx else None
    k_ref = next(it)
    v_ref = next(it)
    g_ref = next(it) if kind == "da" else None
    o_ref = next(it)
    oc_ref = next(it) if with_ctx else None
    vx_ref = next(it)
    i = pl.program_id(2)

    @pl.when(i == 0)
    def _():
        _fill_value_ext(v_ref, vx_ref)

    def attend(q, chunks):
        if kind != "da":
            return _flash(q, k_ref, vx_ref, chunks)
        rows = q.shape[0]
        lane = lax.broadcasted_iota(jnp.int32, q.shape, 1)
        zero = jnp.zeros_like(q)
        qq = jnp.concatenate([jnp.where(lane < DA_HD, q, zero), jnp.where(lane >= DA_HD, q, zero)], axis=0)
        o = _flash(qq, k_ref, vx_ref, chunks)
        od = o[:rows] - sc_ref[0] * o[rows:]
        return (_rms_rows(od) * g_ref[...]) * sc_ref[1]

    @pl.when(i < n_x_tiles)
    def _():
        o_ref[0] = attend(q_ref[0, 0], _key_chunks(0, n_keys)).astype(BF16)

    if with_ctx:
        @pl.when(i >= n_x_tiles)
        def _():
            oc_ref[0] = attend(qc_ref[0, 0], _key_chunks(n_x_keys, n_keys)).astype(BF16)


def _attention(kind, q, k, v, n_x_rows, with_ctx, scal=None, g=None):
    B, H, SA, QW = q.shape
    n_ctx = SA - n_x_rows
    nxt = n_x_rows // ATT_TILE
    kw = dict(kind=kind, n_x_tiles=nxt, n_x_keys=n_x_rows, n_keys=SA, with_ctx=with_ctx)
    x_tile = lambda b, h, i: (b, h, jnp.minimum(i, nxt - 1), 0)
    in_specs, args = [], []
    if kind == "da":
        in_specs.append(pl.BlockSpec(memory_space=pltpu.SMEM))
        args.append(scal)
    in_specs.append(pl.BlockSpec((1, 1, ATT_TILE, QW), x_tile))
    args.append(q)
    if with_ctx:
        in_specs.append(pl.BlockSpec((1, 1, n_ctx, QW), lambda b, h, i: (b, h, n_x_rows // n_ctx, 0)))
        args.append(q)
    in_specs += [pl.BlockSpec((1, 1, SA, QW), lambda b, h, i: (b, h, 0, 0)),
                 pl.BlockSpec((1, 1, SA, HEAD_W), lambda b, h, i: (b, h, 0, 0))]
    args += [k, v]
    if kind == "da":
        in_specs.append(pl.BlockSpec((1, HEAD_W), lambda b, h, i: (0, 0)))
        args.append(g)
    out_shape = [jax.ShapeDtypeStruct((B, n_x_rows, H * HEAD_W), BF16)]
    out_specs = [pl.BlockSpec((1, ATT_TILE, HEAD_W), lambda b, h, i: (b, jnp.minimum(i, nxt - 1), h))]
    if with_ctx:
        out_shape.append(jax.ShapeDtypeStruct((B, n_ctx, H * HEAD_W), BF16))
        out_specs.append(pl.BlockSpec((1, n_ctx, HEAD_W), lambda b, h, i: (b, 0, h)))
    outs = pl.pallas_call(
        functools.partial(_att_kernel, **kw),
        out_shape=tuple(out_shape),
        grid=(B, H, nxt + (1 if with_ctx else 0)),
        in_specs=in_specs,
        out_specs=tuple(out_specs),
        scratch_shapes=[pltpu.VMEM((SA, 2 * HEAD_W), BF16)],
        compiler_params=_cparams(("parallel", "parallel", "arbitrary")),
        name="diff_attention" if kind == "da" else "mla_attention",
    )(*args)
    return (outs[0], outs[1]) if with_ctx else (outs[0], None)


ROUTE_W = 128
NEG_BIG = -3.0e38


def _merge_kernel(*refs, n_x_tiles, with_ctx):
    if with_ctx:
        (x_ref, oda_ref, omla_ref, odac_ref, omlac_ref, gate_ref, mod_ref, g2_ref, wa_ref, wb_ref, wo_ref,
         rw_ref, rb_ref, xo_ref, h2_ref, route_ref, cnt_ref, run_ref) = refs
        is_ctx = pl.program_id(1) >= n_x_tiles
        oda = jnp.where(is_ctx, odac_ref[0], oda_ref[0])
        omla = jnp.where(is_ctx, omlac_ref[0], omla_ref[0])
    else:
        (x_ref, oda_ref, omla_ref, gate_ref, mod_ref, g2_ref, wa_ref, wb_ref, wo_ref,
         rw_ref, rb_ref, xo_ref, h2_ref, route_ref, cnt_ref, run_ref) = refs
        oda = oda_ref[0]
        omla = omla_ref[0]
    D = x_ref.shape[2]
    tm = x_ref.shape[1]
    a = jnp.dot(oda, wa_ref[...], preferred_element_type=F32)
    b = jnp.dot(omla, wb_ref[...], preferred_element_type=F32)
    ga = gate_ref[0, :, 0:D].astype(F32)
    gb = gate_ref[0, :, D:2 * D].astype(F32)
    m = (ga * a + gb * b).astype(BF16)
    mix = jnp.dot(m, wo_ref[...], preferred_element_type=F32)
    xn = x_ref[0] + mod_ref[0, 2:3, :] * mix
    xo_ref[0] = xn
    h2 = (_rms_rows(xn) * g2_ref[...]) * (1.0 + mod_ref[0, 4:5, :]) + mod_ref[0, 3:4, :]
    h2_ref[0] = h2
    lg = jnp.dot(h2.astype(BF16), rw_ref[...], preferred_element_type=F32) + rb_ref[...]

    @pl.when((pl.program_id(0) == 0) & (pl.program_id(1) == 0))
    def _():
        run_ref[...] = jnp.zeros_like(run_ref)

    lane = lax.broadcasted_iota(jnp.int32, lg.shape, 1)
    lanef = lane.astype(F32)
    earlier = (lax.broadcasted_iota(jnp.int32, (tm, tm), 0) > lax.broadcasted_iota(jnp.int32, (tm, tm), 1))
    earlier = jnp.where(earlier, 1.0, 0.0).astype(BF16)
    run = run_ref[0:1, :]
    route = jnp.zeros(lg.shape, F32)
    vals = []
    for r in range(TOP_K):
        mx = jnp.max(lg, axis=-1, keepdims=True)
        idx = jnp.min(jnp.where(lg == mx, lanef, float(ROUTE_W)), axis=-1, keepdims=True)
        hit = lanef == idx
        hitf = jnp.where(hit, 1.0, 0.0)
        before = jnp.dot(earlier, hitf.astype(BF16), preferred_element_type=F32)
        rank = jnp.sum(jnp.where(hit, run + before, 0.0), axis=-1, keepdims=True)
        run = run + jnp.sum(hitf, axis=0, keepdims=True)
        route = jnp.where(lane == r, idx, route)
        route = jnp.where(lane == TOP_K + r, rank, route)
        vals.append(mx)
        lg = jnp.where(hit, NEG_BIG, lg)
    ex = [jnp.exp(v - vals[0]) for v in vals]
    den = ex[0] + ex[1] + ex[2] + ex[3]
    for r in range(TOP_K):
        route = jnp.where(lane == 2 * TOP_K + r, ex[r] / den, route)
    route_ref[0] = route
    run_ref[...] = jnp.broadcast_to(run, run_ref.shape)
    cnt_ref[...] = jnp.broadcast_to(run, cnt_ref.shape)


def _merge(xy, oda, omla, odac, omlac, gates, mod, g2, wa, wb, wo, rw, rb, n_x_tiles, n_q_tiles):
    B, SA, D = xy.shape
    with_ctx = odac is not None
    const2 = lambda b, i: (0, 0)
    row_of = lambda b, i: jnp.where(i < n_x_tiles, b, B)
    row_spec = lambda w: pl.BlockSpec((1, ROW_TILE, w), lambda b, i: (b, i, 0))
    x_rows = lambda w: pl.BlockSpec((1, ROW_TILE, w), lambda b, i: (b, jnp.minimum(i, n_x_tiles - 1), 0))
    c_rows = lambda w: pl.BlockSpec((1, ROW_TILE, w), lambda b, i: (b, 0, 0))
    AW = oda.shape[2]
    att_specs = [x_rows(AW), x_rows(AW)] + ([c_rows(AW), c_rows(AW)] if with_ctx else [])
    att_args = [oda, omla] + ([odac, omlac] if with_ctx else [])
    SQ = n_q_tiles * ROW_TILE
    return pl.pallas_call(
        functools.partial(_merge_kernel, n_x_tiles=n_x_tiles, with_ctx=with_ctx),
        out_shape=(
            jax.ShapeDtypeStruct((B, SQ, D), F32),
            jax.ShapeDtypeStruct((B, SQ, D), F32),
            jax.ShapeDtypeStruct((B, SQ, ROUTE_W), F32),
            jax.ShapeDtypeStruct((8, ROUTE_W), F32),
        ),
        grid=(B, n_q_tiles),
        in_specs=[
            row_spec(D), *att_specs, row_spec(2 * D),
            pl.BlockSpec((1, 6, D), lambda b, i: (row_of(b, i), 0, 0)),
            pl.BlockSpec((1, D), const2),
            pl.BlockSpec(wa.shape, const2), pl.BlockSpec(wb.shape, const2), pl.BlockSpec(wo.shape, const2),
            pl.BlockSpec(rw.shape, const2), pl.BlockSpec(rb.shape, const2),
        ],
        out_specs=(row_spec(D), row_spec(D), row_spec(ROUTE_W), pl.BlockSpec((8, ROUTE_W), const2)),
        scratch_shapes=[pltpu.VMEM((8, ROUTE_W), F32)],
        compiler_params=_cparams(("arbitrary", "arbitrary")),
        name="merge_post",
    )(xy, *att_args, gates, mod, g2, wa, wb, wo, rw, rb)


def _row_copy(src_ref, src_row, dst_ref, dst_row, sem):
    return pltpu.make_async_copy(src_ref.at[pl.ds(src_row, 1)], dst_ref.at[pl.ds(dst_row, 1)], sem)


def _dispatch_kernel(slot_ref, h_ref, buf_in_ref, buf_ref, sem):
    del buf_in_ref
    tm = h_ref.shape[0]

    def issue(r, carry):
        for k in range(TOP_K):
            _row_copy(h_ref, r, buf_ref, slot_ref[0, 0, r * TOP_K + k], sem).start(priority=k % 2)
        return carry

    lax.fori_loop(0, tm, issue, 0, unroll=4)
    for k in range(TOP_K):
        pltpu.make_async_copy(h_ref, buf_ref.at[pl.ds(0, tm)], sem).wait()


def _dispatch(slot3, h2, buf):
    T, D = h2.shape
    nt = T // ROW_TILE
    return pl.pallas_call(
        _dispatch_kernel,
        out_shape=jax.ShapeDtypeStruct(buf.shape, buf.dtype),
        grid=(nt,),
        in_specs=[
            pl.BlockSpec((1, 1, ROW_TILE * TOP_K), lambda t: (t, 0, 0), memory_space=pltpu.SMEM),
            pl.BlockSpec((ROW_TILE, D), lambda t: (t, 0)),
            pl.BlockSpec(memory_space=pl.ANY),
        ],
        out_specs=pl.BlockSpec(memory_space=pl.ANY),
        scratch_shapes=[pltpu.SemaphoreType.DMA(())],
        input_output_aliases={2: 0},
        compiler_params=_cparams(("arbitrary",)),
        name="moe_dispatch",
    )(slot3, h2, buf)


def _expert_kernel(be_ref, nu_ref, x_ref, wgu_ref, bgu_ref, wdn_ref, bdn_ref, o_ref):
    n = pl.program_id(0)

    @pl.when(n < nu_ref[0])
    def _():
        h = jnp.dot(x_ref[...].astype(BF16), wgu_ref[0], preferred_element_type=F32) + bgu_ref[0]
        gate = jnp.minimum(h[:, :D_FF], SWIGLU_LIMIT)
        lin = jnp.clip(h[:, D_FF:], -SWIGLU_LIMIT, SWIGLU_LIMIT)
        act = (lin + 1.0) * gate * jax.nn.sigmoid(SWIGLU_ALPHA * gate)
        o_ref[...] = jnp.dot(act.astype(BF16), wdn_ref[0], preferred_element_type=F32) + bdn_ref[0]

    @pl.when(n >= nu_ref[0])
    def _():
        o_ref[...] = jnp.zeros_like(o_ref)


def _experts(block_e, n_used, buf, wgu, bgu, wdn, bdn):
    R, D = buf.shape
    NB = R // EXPERT_BLOCK
    E = wgu.shape[0]
    return pl.pallas_call(
        _expert_kernel,
        out_shape=jax.ShapeDtypeStruct((R, D), F32),
        grid_spec=pltpu.PrefetchScalarGridSpec(
            num_scalar_prefetch=2,
            grid=(NB,),
            in_specs=[
                pl.BlockSpec((EXPERT_BLOCK, D), lambda n, be, nu: (jnp.minimum(n, nu[0] - 1), 0)),
                pl.BlockSpec((1, D, 2 * D_FF), lambda n, be, nu: (be[n], 0, 0)),
                pl.BlockSpec((1, 1, 2 * D_FF), lambda n, be, nu: (be[n], 0, 0)),
                pl.BlockSpec((1, D_FF, D), lambda n, be, nu: (be[n], 0, 0)),
                pl.BlockSpec((1, 1, D), lambda n, be, nu: (be[n], 0, 0)),
            ],
            out_specs=pl.BlockSpec((EXPERT_BLOCK, D), lambda n, be, nu: (n, 0)),
        ),
        compiler_params=_cparams(("arbitrary",)),
        name="expert_mlp",
    )(block_e, n_used, buf, wgu, bgu.reshape(E, 1, 2 * D_FF), wdn, bdn.reshape(E, 1, D))


def _combine_kernel(slot_ref, nslot_ref, x_ref, route_ref, mod_ref, fg_ref, y_hbm, o_ref, ybuf, sems, *, final):
    t = pl.program_id(0)
    nt = pl.num_programs(0)
    tm = x_ref.shape[0]
    cur = lax.rem(t, 2)

    def gather(s_ref, half):
        def issue(r, carry):
            for k in range(TOP_K):
                _row_copy(y_hbm, s_ref[0, 0, r * TOP_K + k], ybuf.at[half, k], r,
                          sems.at[half]).start(priority=k % 2)
            return carry
        lax.fori_loop(0, tm, issue, 0, unroll=4)

    @pl.when(t == 0)
    def _():
        gather(slot_ref, 0)

    @pl.when(t + 1 < nt)
    def _():
        gather(nslot_ref, 1 - cur)

    for k in range(TOP_K):
        pltpu.make_async_copy(y_hbm.at[pl.ds(0, tm)], ybuf.at[cur, k], sems.at[cur]).wait()

    w = route_ref[:, 2 * TOP_K:3 * TOP_K]
    f = w[:, 0:1] * ybuf[cur, 0]
    for k in range(1, TOP_K):
        f = f + w[:, k:k + 1] * ybuf[cur, k]
    xn = x_ref[...] + mod_ref[0, 5:6, :] * f
    if final:
        xn = _rms_rows(xn) * fg_ref[...]
    o_ref[...] = xn


def _combine(x2, y_sorted, slot3, route, mod, fg, n_x_tiles, n_q_tiles, n_batch, final):
    T, D = x2.shape
    nt = T // ROW_TILE
    row_of = lambda t: jnp.where(t % n_q_tiles < n_x_tiles, t // n_q_tiles, n_batch)
    slot_block = (1, 1, ROW_TILE * TOP_K)
    return pl.pallas_call(
        functools.partial(_combine_kernel, final=final),
        out_shape=jax.ShapeDtypeStruct((T, D), F32),
        grid=(nt,),
        in_specs=[
            pl.BlockSpec(slot_block, lambda t: (t, 0, 0), memory_space=pltpu.SMEM),
            pl.BlockSpec(slot_block, lambda t: (jnp.minimum(t + 1, nt - 1), 0, 0), memory_space=pltpu.SMEM),
            pl.BlockSpec((ROW_TILE, D), lambda t: (t, 0)),
            pl.BlockSpec((ROW_TILE, ROUTE_W), lambda t: (t, 0)),
            pl.BlockSpec((1, 6, D), lambda t: (row_of(t), 0, 0)),
            pl.BlockSpec((1, D), lambda t: (0, 0)),
            pl.BlockSpec(memory_space=pl.ANY),
        ],
        out_specs=pl.BlockSpec((ROW_TILE, D), lambda t: (t, 0)),
        scratch_shapes=[pltpu.VMEM((2, TOP_K, ROW_TILE, D), F32), pltpu.SemaphoreType.DMA((2,))],
        compiler_params=_cparams(("arbitrary",)),
        name="moe_combine",
    )(slot3, slot3, x2, route, mod, fg, y_sorted)


def _block_layout(counts, n_blocks):
    sizes = counts[0, :N_EXPERTS].astype(jnp.int32)
    padded = ((sizes + EXPERT_BLOCK - 1) // EXPERT_BLOCK) * EXPERT_BLOCK
    pend = jnp.cumsum(padded)
    pstart = pend - padded
    first_row = jnp.arange(n_blocks, dtype=jnp.int32) * EXPERT_BLOCK
    block_e = jnp.minimum(jnp.sum(pend[None, :] <= first_row[:, None], axis=1), N_EXPERTS - 1).astype(jnp.int32)
    n_used = (pend[-1] // EXPERT_BLOCK).astype(jnp.int32).reshape(1)
    return pstart, block_e, n_used


def _rope_tables(S, n_ctx):
    rows = jnp.repeat(jnp.arange(S // GRID_W, dtype=F32), GRID_W)
    cols = jnp.tile(jnp.arange(GRID_W, dtype=F32), S // GRID_W)
    half = DA_HD // 2
    freqs = ROPE_BASE ** (-jnp.arange(0, half, 2, dtype=F32) / half)
    ar = rows[:, None] * freqs
    ac = cols[:, None] * freqs
    ang = jnp.concatenate([ar, ar, ac, ac], axis=-1)
    cos = jnp.concatenate([jnp.cos(ang), jnp.ones((n_ctx, DA_HD), F32)], axis=0)
    sin = jnp.concatenate([jnp.sin(ang), jnp.zeros((n_ctx, DA_HD), F32)], axis=0)
    cos = jnp.tile(cos, (1, HEAD_W // DA_HD))
    sin = jnp.tile(sin, (1, HEAD_W // DA_HD))
    lo = (jnp.arange(HEAD_W) % (DA_HD // 2)) < (DA_HD // 4)
    return cos, jnp.where(lo, -sin, 0.0), jnp.where(lo, 0.0, sin)


def kernel(x, c, ctx, c_ctx, ada_w, ada_b, norm1_g, norm2_g, w_in, da_lq1, da_lk1, da_lq2, da_lk2, da_subln_g, mla_qa_g, mla_kva_g, mla_wqb, mla_wkvb, w_br_a, w_br_b, w_out, router_w, router_b, moe_wgu, moe_bgu, moe_wdn, moe_bdn, final_g):
    B, S, D = x.shape
    CT = ctx.shape[1]
    L = ada_w.shape[0]
    SA = S + CT
    assert S % ATT_TILE == 0 and CT == ROW_TILE and S % GRID_W == 0
    n_x_tiles = S // ROW_TILE
    n_tiles = SA // ROW_TILE
    H = DA_HEADS

    xy = jnp.concatenate([x, ctx], axis=1)
    cc = jnp.concatenate([c, c_ctx[None, :], jnp.zeros((16 - B - 1, D), F32)], axis=0)
    mod_all = _ada_all(cc, ada_w, ada_b).reshape(L, 16, 6, D)
    cos, sa, sb = _rope_tables(S, CT)

    qw = H * HEAD_W
    o_cq = 3 * qw
    o_kr = o_cq + MLA_Q_RANK + MLA_KV_RANK
    o_g = o_kr + MLA_ROPE
    wm = w_in[:, :, :o_kr].astype(BF16)
    wkr = jnp.pad(w_in[:, :, o_kr:o_g], ((0, 0), (0, 0), (0, HEAD_W - MLA_ROPE))).astype(BF16)
    wg = w_in[:, :, o_g:].astype(BF16)
    wqb4 = mla_wqb.reshape(L, MLA_Q_RANK, MLA_HEADS, MLA_NOPE + MLA_ROPE)
    wqb_n = wqb4[..., :MLA_NOPE].reshape(L, MLA_Q_RANK, MLA_HEADS * MLA_NOPE)
    wqb_r = jnp.pad(wqb4[..., MLA_NOPE:], ((0, 0), (0, 0), (0, 0), (0, HEAD_W - MLA_ROPE)))
    wqb = jnp.concatenate([wqb_n, wqb_r.reshape(L, MLA_Q_RANK, MLA_HEADS * HEAD_W)], axis=-1).astype(BF16)
    wkvb4 = mla_wkvb.reshape(L, MLA_KV_RANK, MLA_HEADS, MLA_NOPE + MLA_V)
    wkvb = jnp.concatenate([wkvb4[..., :MLA_NOPE].reshape(L, MLA_KV_RANK, -1),
                            wkvb4[..., MLA_NOPE:].reshape(L, MLA_KV_RANK, -1)], axis=-1).astype(BF16)
    wa = w_br_a.astype(BF16)
    wb = w_br_b.astype(BF16)
    wo = w_out.astype(BF16)
    RW = ROUTE_W
    rw = jnp.pad(router_w, ((0, 0), (0, 0), (0, RW - N_EXPERTS))).astype(BF16)
    rb = jnp.pad(router_b, ((0, 0), (0, RW - N_EXPERTS)), constant_values=NEG_BIG)
    wgu = moe_wgu.astype(BF16)
    wdn = moe_wdn.astype(BF16)
    NB = (B * SA * TOP_K + EXPERT_BLOCK - 1) // EXPERT_BLOCK + N_EXPERTS
    buf = jnp.zeros((NB * EXPERT_BLOCK, D), F32)

    for l in range(L):
        last = l == L - 1
        n_q = n_x_tiles if last else n_tiles
        mod = mod_all[l]
        lam_init = 0.8 - 0.6 * math.exp(-0.3 * l)
        lam = (jnp.exp(jnp.sum(da_lq1[l] * da_lk1[l])) - jnp.exp(jnp.sum(da_lq2[l] * da_lk2[l])) + lam_init)
        scal = jnp.stack([lam, jnp.asarray(1.0 - lam_init, F32)]).astype(F32)

        daq, dak, dav, mq, mk, mv, gates = _pre(
            xy, mod, norm1_g[l][None], cos, sa, sb, wm[l], wkr[l], wg[l],
            mla_qa_g[l][None], mla_kva_g[l][None], wqb[l], wkvb[l], n_x_tiles)
        oda, odac = _attention("da", daq, dak, dav, S, not last, scal=scal, g=da_subln_g[l][None])
        omla, omlac = _attention("mla", mq, mk, mv, S, not last)
        xy2, h2, route, counts = _merge(xy, oda, omla, odac, omlac, gates, mod, norm2_g[l][None], wa[l], wb[l], wo[l],
                                 rw[l], rb[l][None], n_x_tiles, n_q)

        SQ = n_q * ROW_TILE
        T = B * SQ
        pstart, block_e, n_used = _block_layout(counts, NB)
        route2 = route.reshape(T, ROUTE_W)
        e_idx = route2[:, 0:TOP_K].astype(jnp.int32)
        rank = route2[:, TOP_K:2 * TOP_K].astype(jnp.int32)
        base = jnp.sum(jnp.where(e_idx[:, :, None] == jnp.arange(N_EXPERTS, dtype=jnp.int32), pstart, 0), axis=-1)
        slot3 = (base + rank).reshape(T // ROW_TILE, 1, ROW_TILE * TOP_K)
        buf = _dispatch(slot3, h2.reshape(T, D), buf)
        out_sorted = _experts(block_e, n_used, buf, wgu[l], moe_bgu[l], wdn[l], moe_bdn[l])
        xy = _combine(xy2.reshape(T, D), out_sorted, slot3, route2, mod, final_g[None],
                      n_x_tiles, n_q, B, last).reshape(B, SQ, D)
    return xy
```

```python
import functools
import math

import jax
import jax.numpy as jnp
from jax import lax
from jax.experimental import pallas as pl
from jax.experimental.pallas import tpu as pltpu

F32 = jnp.float32
BF16 = jnp.bfloat16

EPS = 1e-6
GRID_W = 64
ROPE_BASE = 10000.0
DA_HEADS = 4
DA_HD = 64
MLA_HEADS = 4
MLA_NOPE = 128
MLA_ROPE = 64
MLA_V = 128
MLA_Q_RANK = 384
MLA_KV_RANK = 256
N_EXPERTS = 32
TOP_K = 4
D_FF = 512
SWIGLU_LIMIT = 7.0
SWIGLU_ALPHA = 1.702
EXPERT_BLOCK = 512

LOG2E = math.log2(math.e)
HEAD_W = 128
ROW_TILE = 256
ATT_TILE = 512
KEY_CHUNK = 512
VMEM_LIMIT = 56 * 1024 * 1024


def _cparams(sem):
    return pltpu.CompilerParams(dimension_semantics=sem, vmem_limit_bytes=VMEM_LIMIT)


def _rms_rows(x):
    return x * lax.rsqrt(jnp.mean(x * x, axis=-1, keepdims=True) + EPS)


def _pack_halves(x):
    n = x.shape[1] // 2
    lo = lax.bitcast_convert_type(x[:, :n].astype(BF16).astype(F32), jnp.uint32)
    hi = lax.bitcast_convert_type(x[:, n:].astype(BF16).astype(F32), jnp.uint32)
    return lax.shift_right_logical(lo, jnp.uint32(16)) | hi


def _unpack_halves(w):
    lo = lax.bitcast_convert_type(lax.shift_left(w, jnp.uint32(16)), F32)
    hi = lax.bitcast_convert_type(w & jnp.uint32(0xFFFF0000), F32)
    return lo, hi


def _ada_kernel(c_ref, w_ref, b_ref, o_ref):
    c = c_ref[...]
    s = (c * jax.nn.sigmoid(c)).astype(BF16)
    o_ref[0] = jnp.dot(s, w_ref[0].astype(BF16), preferred_element_type=F32) + b_ref[0]


def _ada_all(cc, ada_w, ada_b):
    L, D, D6 = ada_w.shape
    R = cc.shape[0]
    nj = D6 // D
    return pl.pallas_call(
        _ada_kernel,
        out_shape=jax.ShapeDtypeStruct((L, R, D6), F32),
        grid=(L, nj),
        in_specs=[
            pl.BlockSpec((R, D), lambda l, j: (0, 0)),
            pl.BlockSpec((1, D, D), lambda l, j: (l, 0, j)),
            pl.BlockSpec((1, 1, D), lambda l, j: (l, 0, j)),
        ],
        out_specs=pl.BlockSpec((1, R, D), lambda l, j: (l, 0, j)),
        compiler_params=_cparams(("arbitrary", "arbitrary")),
        name="ada_mod",
    )(cc, ada_w, ada_b.reshape(L, 1, D6))


def _rope(z, cos, sin_a, sin_b):
    return z * cos + pltpu.roll(z, 112, 1) * sin_a + pltpu.roll(z, 16, 1) * sin_b


def _pre_kernel(x_ref, mod_ref, g_ref, cos_ref, sa_ref, sb_ref, wm_ref, wkr_ref, wg_ref,
                qag_ref, kvag_ref, wqb_ref, wkvb_ref,
                daq_ref, dak_ref, dav_ref, mq_ref, mk_ref, mv_ref, gate_ref, *, da_scale, mla_scale):
    x = x_ref[0]
    shift = mod_ref[0, 0:1, :]
    scale = mod_ref[0, 1:2, :]
    h = ((_rms_rows(x) * g_ref[...]) * (1.0 + scale) + shift).astype(BF16)
    cos = cos_ref[...]
    sa = sa_ref[...]
    sb = sb_ref[...]
    nh = DA_HEADS
    qw = nh * HEAD_W

    zq = jnp.dot(h, wm_ref[:, 0:qw], preferred_element_type=F32)
    for hd in range(nh):
        z = _rope(zq[:, hd * HEAD_W:(hd + 1) * HEAD_W], cos, sa, sb) * da_scale
        daq_ref[0, hd] = z.astype(BF16)
    zk = jnp.dot(h, wm_ref[:, qw:2 * qw], preferred_element_type=F32)
    for hd in range(nh):
        z = _rope(zk[:, hd * HEAD_W:(hd + 1) * HEAD_W], cos, sa, sb)
        dak_ref[0, hd] = z.astype(BF16)
    zv = jnp.dot(h, wm_ref[:, 2 * qw:3 * qw], preferred_element_type=F32)
    for hd in range(nh):
        dav_ref[0, hd] = zv[:, hd * HEAD_W:(hd + 1) * HEAD_W].astype(BF16)

    o0 = 3 * qw
    cq = jnp.dot(h, wm_ref[:, o0:o0 + MLA_Q_RANK], preferred_element_type=F32)
    cqn = (_rms_rows(cq) * qag_ref[...]).astype(BF16)
    qm = jnp.dot(cqn, wqb_ref[...], preferred_element_type=F32)
    for hd in range(MLA_HEADS):
        mq_ref[0, hd, :, 0:HEAD_W] = (qm[:, hd * HEAD_W:(hd + 1) * HEAD_W] * mla_scale).astype(BF16)
        zr = qm[:, (MLA_HEADS + hd) * HEAD_W:(MLA_HEADS + hd + 1) * HEAD_W]
        mq_ref[0, hd, :, HEAD_W:2 * HEAD_W] = (_rope(zr, cos, sa, sb) * mla_scale).astype(BF16)

    o1 = o0 + MLA_Q_RANK
    ckv = jnp.dot(h, wm_ref[:, o1:o1 + MLA_KV_RANK], preferred_element_type=F32)
    ckvn = (_rms_rows(ckv) * kvag_ref[...]).astype(BF16)
    kv = jnp.dot(ckvn, wkvb_ref[...], preferred_element_type=F32)
    zkr = jnp.dot(h, wkr_ref[...], preferred_element_type=F32)
    kr = _rope(zkr, cos, sa, sb).astype(BF16)
    for hd in range(MLA_HEADS):
        mk_ref[0, hd, :, 0:HEAD_W] = kv[:, hd * HEAD_W:(hd + 1) * HEAD_W].astype(BF16)
        mk_ref[0, hd, :, HEAD_W:2 * HEAD_W] = kr
        mv_ref[0, hd] = kv[:, (MLA_HEADS + hd) * HEAD_W:(MLA_HEADS + hd + 1) * HEAD_W].astype(BF16)

    gz = jnp.dot(h, wg_ref[...], preferred_element_type=F32)
    gate_ref[0] = jax.nn.sigmoid(gz).astype(BF16)


def _pre(xy, mod, g1, cos, sa, sb, wm, wkr, wg, qag, kvag, wqb, wkvb, n_x_tiles):
    B, SA, D = xy.shape
    nt = SA // ROW_TILE
    H = DA_HEADS
    const2 = lambda b, i: (0, 0)
    head_spec = lambda w: pl.BlockSpec((1, H, ROW_TILE, w), lambda b, i: (b, 0, i, 0))
    row_of = lambda b, i: jnp.where(i < n_x_tiles, b, B)
    kern = functools.partial(_pre_kernel, da_scale=LOG2E / math.sqrt(DA_HD),
                             mla_scale=LOG2E / math.sqrt(MLA_NOPE + MLA_ROPE))
    return pl.pallas_call(
        kern,
        out_shape=(
            jax.ShapeDtypeStruct((B, H, SA, HEAD_W), BF16),
            jax.ShapeDtypeStruct((B, H, SA, HEAD_W), BF16),
            jax.ShapeDtypeStruct((B, H, SA, HEAD_W), BF16),
            jax.ShapeDtypeStruct((B, H, SA, 2 * HEAD_W), BF16),
            jax.ShapeDtypeStruct((B, H, SA, 2 * HEAD_W), BF16),
            jax.ShapeDtypeStruct((B, H, SA, HEAD_W), BF16),
            jax.ShapeDtypeStruct((B, SA, 2 * D), BF16),
        ),
        grid=(B, nt),
        in_specs=[
            pl.BlockSpec((1, ROW_TILE, D), lambda b, i: (b, i, 0)),
            pl.BlockSpec((1, 6, D), lambda b, i: (row_of(b, i), 0, 0)),
            pl.BlockSpec((1, D), const2),
            pl.BlockSpec((ROW_TILE, HEAD_W), lambda b, i: (i, 0)),
            pl.BlockSpec((ROW_TILE, HEAD_W), lambda b, i: (i, 0)),
            pl.BlockSpec((ROW_TILE, HEAD_W), lambda b, i: (i, 0)),
            pl.BlockSpec(wm.shape, const2),
            pl.BlockSpec(wkr.shape, const2),
            pl.BlockSpec(wg.shape, const2),
            pl.BlockSpec(qag.shape, const2),
            pl.BlockSpec(kvag.shape, const2),
            pl.BlockSpec(wqb.shape, const2),
            pl.BlockSpec(wkvb.shape, const2),
        ],
        out_specs=(
            head_spec(HEAD_W), head_spec(HEAD_W), head_spec(HEAD_W),
            head_spec(2 * HEAD_W), head_spec(2 * HEAD_W), head_spec(HEAD_W),
            pl.BlockSpec((1, ROW_TILE, 2 * D), lambda b, i: (b, i, 0)),
        ),
        compiler_params=_cparams(("parallel", "arbitrary")),
        name="pre_mixer",
    )(xy, mod, g1, cos, sa, sb, wm, wkr, wg, qag, kvag, wqb, wkvb)


def _key_chunks(start, stop):
    out = []
    while start < stop:
        size = min(KEY_CHUNK, stop - start)
        out.append((start, size))
        start += size
    return out


def _fill_value_ext(v_ref, vx_ref):
    vx_ref[:, 0:HEAD_W] = v_ref[0, 0]
    vx_ref[:, HEAD_W:2 * HEAD_W] = jnp.ones((v_ref.shape[2], HEAD_W), BF16)


def _flash(qq, k_ref, vx_ref, chunks):
    m = None
    acc = None
    for (st, sz) in chunks:
        s = lax.dot_general(qq, k_ref[0, 0, st:st + sz, :], (((1,), (1,)), ((), ())),
                            preferred_element_type=F32)
        mc = jnp.max(s, axis=-1, keepdims=True)
        m_new = mc if m is None else jnp.maximum(m, mc)
        e = jnp.exp2(s - m_new)
        pc = jnp.dot(e.astype(BF16), vx_ref[st:st + sz, :], preferred_element_type=F32)
        acc = pc if m is None else jnp.exp2(m - m_new) * acc + pc
        m = m_new
    return acc[:, :HEAD_W] / acc[:, HEAD_W:HEAD_W + 1]


def _att_kernel(*refs, kind, n_x_tiles, n_x_keys, n_keys, with_ctx):
    it = iter(refs)
    sc_ref = next(it) if kind == "da" else None
    q_ref = next(it)
    qc_ref = next(it) if with_ctx else None
    k_ref = next(it)
    v_ref = next(it)
    g_ref = next(it) if kind == "da" else None
    o_ref = next(it)
    oc_ref = next(it) if with_ctx else None
    vx_ref = next(it)
    i = pl.program_id(2)

    @pl.when(i == 0)
    def _():
        _fill_value_ext(v_ref, vx_ref)

    def attend(q, chunks):
        if kind != "da":
            return _flash(q, k_ref, vx_ref, chunks)
        rows = q.shape[0]
        lane = lax.broadcasted_iota(jnp.int32, q.shape, 1)
        zero = jnp.zeros_like(q)
        qq = jnp.concatenate([jnp.where(lane < DA_HD, q, zero), jnp.where(lane >= DA_HD, q, zero)], axis=0)
        o = _flash(qq, k_ref, vx_ref, chunks)
        od = o[:rows] - sc_ref[0] * o[rows:]
        return (_rms_rows(od) * g_ref[...]) * sc_ref[1]

    @pl.when(i < n_x_tiles)
    def _():
        o_ref[0] = attend(q_ref[0, 0], _key_chunks(0, n_keys)).astype(BF16)

    if with_ctx:
        @pl.when(i >= n_x_tiles)
        def _():
            oc_ref[0] = attend(qc_ref[0, 0], _key_chunks(n_x_keys, n_keys)).astype(BF16)


def _attention(kind, q, k, v, n_x_rows, with_ctx, scal=None, g=None):
    B, H, SA, QW = q.shape
    n_ctx = SA - n_x_rows
    nxt = n_x_rows // ATT_TILE
    kw = dict(kind=kind, n_x_tiles=nxt, n_x_keys=n_x_rows, n_keys=SA, with_ctx=with_ctx)
    x_tile = lambda b, h, i: (b, h, jnp.minimum(i, nxt - 1), 0)
    in_specs, args = [], []
    if kind == "da":
        in_specs.append(pl.BlockSpec(memory_space=pltpu.SMEM))
        args.append(scal)
    in_specs.append(pl.BlockSpec((1, 1, ATT_TILE, QW), x_tile))
    args.append(q)
    if with_ctx:
        in_specs.append(pl.BlockSpec((1, 1, n_ctx, QW), lambda b, h, i: (b, h, n_x_rows // n_ctx, 0)))
        args.append(q)
    in_specs += [pl.BlockSpec((1, 1, SA, QW), lambda b, h, i: (b, h, 0, 0)),
                 pl.BlockSpec((1, 1, SA, HEAD_W), lambda b, h, i: (b, h, 0, 0))]
    args += [k, v]
    if kind == "da":
        in_specs.append(pl.BlockSpec((1, HEAD_W), lambda b, h, i: (0, 0)))
        args.append(g)
    out_shape = [jax.ShapeDtypeStruct((B, n_x_rows, H * HEAD_W), BF16)]
    out_specs = [pl.BlockSpec((1, ATT_TILE, HEAD_W), lambda b, h, i: (b, jnp.minimum(i, nxt - 1), h))]
    if with_ctx:
        out_shape.append(jax.ShapeDtypeStruct((B, n_ctx, H * HEAD_W), BF16))
        out_specs.append(pl.BlockSpec((1, n_ctx, HEAD_W), lambda b, h, i: (b, 0, h)))
    outs = pl.pallas_call(
        functools.partial(_att_kernel, **kw),
        out_shape=tuple(out_shape),
        grid=(B, H, nxt + (1 if with_ctx else 0)),
        in_specs=in_specs,
        out_specs=tuple(out_specs),
        scratch_shapes=[pltpu.VMEM((SA, 2 * HEAD_W), BF16)],
        compiler_params=_cparams(("parallel", "parallel", "arbitrary")),
        name="diff_attention" if kind == "da" else "mla_attention",
    )(*args)
    return (outs[0], outs[1]) if with_ctx else (outs[0], None)


ROUTE_W = 128
NEG_BIG = -3.0e38


def _merge_kernel(*refs, n_x_tiles, with_ctx):
    if with_ctx:
        (x_ref, oda_ref, omla_ref, odac_ref, omlac_ref, gate_ref, mod_ref, g2_ref, wa_ref, wb_ref, wo_ref,
         rw_ref, rb_ref, xo_ref, h2_ref, route_ref, cnt_ref, run_ref) = refs
        is_ctx = pl.program_id(1) >= n_x_tiles
        oda = jnp.where(is_ctx, odac_ref[0], oda_ref[0])
        omla = jnp.where(is_ctx, omlac_ref[0], omla_ref[0])
    else:
        (x_ref, oda_ref, omla_ref, gate_ref, mod_ref, g2_ref, wa_ref, wb_ref, wo_ref,
         rw_ref, rb_ref, xo_ref, h2_ref, route_ref, cnt_ref, run_ref) = refs
        oda = oda_ref[0]
        omla = omla_ref[0]
    D = x_ref.shape[2]
    tm = x_ref.shape[1]
    a = jnp.dot(oda, wa_ref[...], preferred_element_type=F32)
    b = jnp.dot(omla, wb_ref[...], preferred_element_type=F32)
    ga = gate_ref[0, :, 0:D].astype(F32)
    gb = gate_ref[0, :, D:2 * D].astype(F32)
    m = (ga * a + gb * b).astype(BF16)
    mix = jnp.dot(m, wo_ref[...], preferred_element_type=F32)
    xn = x_ref[0] + mod_ref[0, 2:3, :] * mix
    xo_ref[0] = xn
    h2 = (_rms_rows(xn) * g2_ref[...]) * (1.0 + mod_ref[0, 4:5, :]) + mod_ref[0, 3:4, :]
    h2_ref[0] = _pack_halves(h2)
    lg = jnp.dot(h2.astype(BF16), rw_ref[...], preferred_element_type=F32) + rb_ref[...]

    @pl.when((pl.program_id(0) == 0) & (pl.program_id(1) == 0))
    def _():
        run_ref[...] = jnp.zeros_like(run_ref)

    lane = lax.broadcasted_iota(jnp.int32, lg.shape, 1)
    lanef = lane.astype(F32)
    earlier = (lax.broadcasted_iota(jnp.int32, (tm, tm), 0) > lax.broadcasted_iota(jnp.int32, (tm, tm), 1))
    earlier = jnp.where(earlier, 1.0, 0.0).astype(BF16)
    run = run_ref[0:1, :]
    route = jnp.zeros(lg.shape, F32)
    vals = []
    for r in range(TOP_K):
        mx = jnp.max(lg, axis=-1, keepdims=True)
        idx = jnp.min(jnp.where(lg == mx, lanef, float(ROUTE_W)), axis=-1, keepdims=True)
        hit = lanef == idx
        hitf = jnp.where(hit, 1.0, 0.0)
        before = jnp.dot(earlier, hitf.astype(BF16), preferred_element_type=F32)
        rank = jnp.sum(jnp.where(hit, run + before, 0.0), axis=-1, keepdims=True)
        run = run + jnp.sum(hitf, axis=0, keepdims=True)
        route = jnp.where(lane == r, idx, route)
        route = jnp.where(lane == TOP_K + r, rank, route)
        vals.append(mx)
        lg = jnp.where(hit, NEG_BIG, lg)
    ex = [jnp.exp(v - vals[0]) for v in vals]
    den = ex[0] + ex[1] + ex[2] + ex[3]
    for r in range(TOP_K):
        route = jnp.where(lane == 2 * TOP_K + r, ex[r] / den, route)
    route_ref[0] = route
    run_ref[...] = jnp.broadcast_to(run, run_ref.shape)
    cnt_ref[...] = jnp.broadcast_to(run, cnt_ref.shape)


def _merge(xy, oda, omla, odac, omlac, gates, mod, g2, wa, wb, wo, rw, rb, n_x_tiles, n_q_tiles):
    B, SA, D = xy.shape
    with_ctx = odac is not None
    const2 = lambda b, i: (0, 0)
    row_of = lambda b, i: jnp.where(i < n_x_tiles, b, B)
    row_spec = lambda w: pl.BlockSpec((1, ROW_TILE, w), lambda b, i: (b, i, 0))
    x_rows = lambda w: pl.BlockSpec((1, ROW_TILE, w), lambda b, i: (b, jnp.minimum(i, n_x_tiles - 1), 0))
    c_rows = lambda w: pl.BlockSpec((1, ROW_TILE, w), lambda b, i: (b, 0, 0))
    AW = oda.shape[2]
    att_specs = [x_rows(AW), x_rows(AW)] + ([c_rows(AW), c_rows(AW)] if with_ctx else [])
    att_args = [oda, omla] + ([odac, omlac] if with_ctx else [])
    SQ = n_q_tiles * ROW_TILE
    return pl.pallas_call(
        functools.partial(_merge_kernel, n_x_tiles=n_x_tiles, with_ctx=with_ctx),
        out_shape=(
            jax.ShapeDtypeStruct((B, SQ, D), F32),
            jax.ShapeDtypeStruct((B, SQ, D // 2), jnp.uint32),
            jax.ShapeDtypeStruct((B, SQ, ROUTE_W), F32),
            jax.ShapeDtypeStruct((8, ROUTE_W), F32),
        ),
        grid=(B, n_q_tiles),
        in_specs=[
            row_spec(D), *att_specs, row_spec(2 * D),
            pl.BlockSpec((1, 6, D), lambda b, i: (row_of(b, i), 0, 0)),
            pl.BlockSpec((1, D), const2),
            pl.BlockSpec(wa.shape, const2), pl.BlockSpec(wb.shape, const2), pl.BlockSpec(wo.shape, const2),
            pl.BlockSpec(rw.shape, const2), pl.BlockSpec(rb.shape, const2),
        ],
        out_specs=(row_spec(D), row_spec(D // 2), row_spec(ROUTE_W), pl.BlockSpec((8, ROUTE_W), const2)),
        scratch_shapes=[pltpu.VMEM((8, ROUTE_W), F32)],
        compiler_params=_cparams(("arbitrary", "arbitrary")),
        name="merge_post",
    )(xy, *att_args, gates, mod, g2, wa, wb, wo, rw, rb)


def _row_copy(src_ref, src_row, dst_ref, dst_row, sem):
    return pltpu.make_async_copy(src_ref.at[pl.ds(src_row, 1)], dst_ref.at[pl.ds(dst_row, 1)], sem)


def _dispatch_kernel(slot_ref, h_ref, buf_in_ref, buf_ref, sem):
    del buf_in_ref
    tm = h_ref.shape[0]

    for r in range(tm):
        for k in range(TOP_K):
            _row_copy(h_ref, r, buf_ref, slot_ref[0, 0, r * TOP_K + k], sem).start(priority=k % 2)
    for k in range(TOP_K):
        pltpu.make_async_copy(h_ref, buf_ref.at[pl.ds(0, tm)], sem).wait()


def _dispatch(slot3, h2, buf):
    T, D = h2.shape
    nt = T // ROW_TILE
    return pl.pallas_call(
        _dispatch_kernel,
        out_shape=jax.ShapeDtypeStruct(buf.shape, buf.dtype),
        grid=(nt,),
        in_specs=[
            pl.BlockSpec((1, 1, ROW_TILE * TOP_K), lambda t: (t, 0, 0), memory_space=pltpu.SMEM),
            pl.BlockSpec((ROW_TILE, D), lambda t: (t, 0)),
            pl.BlockSpec(memory_space=pl.ANY),
        ],
        out_specs=pl.BlockSpec(memory_space=pl.ANY),
        scratch_shapes=[pltpu.SemaphoreType.DMA(())],
        input_output_aliases={2: 0},
        compiler_params=_cparams(("arbitrary",)),
        name="moe_dispatch",
    )(slot3, h2, buf)


def _expert_kernel(be_ref, nu_ref, x_ref, wgu_ref, bgu_ref, wdn_ref, bdn_ref, o_ref):
    n = pl.program_id(0)

    @pl.when(n < nu_ref[0])
    def _():
        xa, xb = _unpack_halves(x_ref[...])
        half = xa.shape[1]
        h = (jnp.dot(xa.astype(BF16), wgu_ref[0, 0:half, :], preferred_element_type=F32)
             + jnp.dot(xb.astype(BF16), wgu_ref[0, half:2 * half, :], preferred_element_type=F32) + bgu_ref[0])
        gate = jnp.minimum(h[:, :D_FF], SWIGLU_LIMIT)
        lin = jnp.clip(h[:, D_FF:], -SWIGLU_LIMIT, SWIGLU_LIMIT)
        act = (lin + 1.0) * gate * jax.nn.sigmoid(SWIGLU_ALPHA * gate)
        o_ref[...] = _pack_halves(jnp.dot(act.astype(BF16), wdn_ref[0], preferred_element_type=F32) + bdn_ref[0])

    @pl.when(n >= nu_ref[0])
    def _():
        o_ref[...] = jnp.zeros_like(o_ref)


def _experts(block_e, n_used, buf, wgu, bgu, wdn, bdn):
    R, DP = buf.shape
    D = 2 * DP
    NB = R // EXPERT_BLOCK
    E = wgu.shape[0]
    return pl.pallas_call(
        _expert_kernel,
        out_shape=jax.ShapeDtypeStruct((R, DP), jnp.uint32),
        grid_spec=pltpu.PrefetchScalarGridSpec(
            num_scalar_prefetch=2,
            grid=(NB,),
            in_specs=[
                pl.BlockSpec((EXPERT_BLOCK, DP), lambda n, be, nu: (jnp.minimum(n, nu[0] - 1), 0)),
                pl.BlockSpec((1, D, 2 * D_FF), lambda n, be, nu: (be[n], 0, 0)),
                pl.BlockSpec((1, 1, 2 * D_FF), lambda n, be, nu: (be[n], 0, 0)),
                pl.BlockSpec((1, D_FF, D), lambda n, be, nu: (be[n], 0, 0)),
                pl.BlockSpec((1, 1, D), lambda n, be, nu: (be[n], 0, 0)),
            ],
            out_specs=pl.BlockSpec((EXPERT_BLOCK, DP), lambda n, be, nu: (n, 0)),
        ),
        compiler_params=_cparams(("arbitrary",)),
        name="expert_mlp",
    )(block_e, n_used, buf, wgu, bgu.reshape(E, 1, 2 * D_FF), wdn, bdn.reshape(E, 1, D))


def _combine_kernel(slot_ref, nslot_ref, x_ref, route_ref, mod_ref, fg_ref, y_hbm, o_ref, ybuf, sems, *, final):
    t = pl.program_id(0)
    nt = pl.num_programs(0)
    tm = x_ref.shape[0]
    cur = lax.rem(t, 2)

    def gather(s_ref, half):
        for r in range(tm):
            for k in range(TOP_K):
                _row_copy(y_hbm, s_ref[0, 0, r * TOP_K + k], ybuf.at[half, k], r,
                          sems.at[half]).start(priority=k % 2)

    @pl.when(t == 0)
    def _():
        gather(slot_ref, 0)

    @pl.when(t + 1 < nt)
    def _():
        gather(nslot_ref, 1 - cur)

    for k in range(TOP_K):
        pltpu.make_async_copy(y_hbm.at[pl.ds(0, tm)], ybuf.at[cur, k], sems.at[cur]).wait()

    w = route_ref[:, 2 * TOP_K:3 * TOP_K]
    f_lo = f_hi = None
    for k in range(TOP_K):
        y_lo, y_hi = _unpack_halves(ybuf[cur, k])
        f_lo = w[:, k:k + 1] * y_lo if k == 0 else f_lo + w[:, k:k + 1] * y_lo
        f_hi = w[:, k:k + 1] * y_hi if k == 0 else f_hi + w[:, k:k + 1] * y_hi
    f = jnp.concatenate([f_lo, f_hi], axis=-1)
    xn = x_ref[...] + mod_ref[0, 5:6, :] * f
    if final:
        xn = _rms_rows(xn) * fg_ref[...]
    o_ref[...] = xn


def _combine(x2, y_sorted, slot3, route, mod, fg, n_x_tiles, n_q_tiles, n_batch, final):
    T, D = x2.shape
    nt = T // ROW_TILE
    row_of = lambda t: jnp.where(t % n_q_tiles < n_x_tiles, t // n_q_tiles, n_batch)
    slot_block = (1, 1, ROW_TILE * TOP_K)
    return pl.pallas_call(
        functools.partial(_combine_kernel, final=final),
        out_shape=jax.ShapeDtypeStruct((T, D), F32),
        grid=(nt,),
        in_specs=[
            pl.BlockSpec(slot_block, lambda t: (t, 0, 0), memory_space=pltpu.SMEM),
            pl.BlockSpec(slot_block, lambda t: (jnp.minimum(t + 1, nt - 1), 0, 0), memory_space=pltpu.SMEM),
            pl.BlockSpec((ROW_TILE, D), lambda t: (t, 0)),
            pl.BlockSpec((ROW_TILE, ROUTE_W), lambda t: (t, 0)),
            pl.BlockSpec((1, 6, D), lambda t: (row_of(t), 0, 0)),
            pl.BlockSpec((1, D), lambda t: (0, 0)),
            pl.BlockSpec(memory_space=pl.ANY),
        ],
        out_specs=pl.BlockSpec((ROW_TILE, D), lambda t: (t, 0)),
        scratch_shapes=[pltpu.VMEM((2, TOP_K, ROW_TILE, D // 2), jnp.uint32), pltpu.SemaphoreType.DMA((2,))],
        compiler_params=_cparams(("arbitrary",)),
        name="moe_combine",
    )(slot3, slot3, x2, route, mod, fg, y_sorted)


def _block_layout(counts, n_blocks):
    sizes = counts[0, :N_EXPERTS].astype(jnp.int32)
    padded = ((sizes + EXPERT_BLOCK - 1) // EXPERT_BLOCK) * EXPERT_BLOCK
    pend = jnp.cumsum(padded)
    pstart = pend - padded
    first_row = jnp.arange(n_blocks, dtype=jnp.int32) * EXPERT_BLOCK
    block_e = jnp.minimum(jnp.sum(pend[None, :] <= first_row[:, None], axis=1), N_EXPERTS - 1).astype(jnp.int32)
    n_used = (pend[-1] // EXPERT_BLOCK).astype(jnp.int32).reshape(1)
    return pstart, block_e, n_used


def _rope_tables(S, n_ctx):
    rows = jnp.repeat(jnp.arange(S // GRID_W, dtype=F32), GRID_W)
    cols = jnp.tile(jnp.arange(GRID_W, dtype=F32), S // GRID_W)
    half = DA_HD // 2
    freqs = ROPE_BASE ** (-jnp.arange(0, half, 2, dtype=F32) / half)
    ar = rows[:, None] * freqs
    ac = cols[:, None] * freqs
    ang = jnp.concatenate([ar, ar, ac, ac], axis=-1)
    cos = jnp.concatenate([jnp.cos(ang), jnp.ones((n_ctx, DA_HD), F32)], axis=0)
    sin = jnp.concatenate([jnp.sin(ang), jnp.zeros((n_ctx, DA_HD), F32)], axis=0)
    cos = jnp.tile(cos, (1, HEAD_W // DA_HD))
    sin = jnp.tile(sin, (1, HEAD_W // DA_HD))
    lo = (jnp.arange(HEAD_W) % (DA_HD // 2)) < (DA_HD // 4)
    return cos, jnp.where(lo, -sin, 0.0), jnp.where(lo, 0.0, sin)


def kernel(x, c, ctx, c_ctx, ada_w, ada_b, norm1_g, norm2_g, w_in, da_lq1, da_lk1, da_lq2, da_lk2, da_subln_g, mla_qa_g, mla_kva_g, mla_wqb, mla_wkvb, w_br_a, w_br_b, w_out, router_w, router_b, moe_wgu, moe_bgu, moe_wdn, moe_bdn, final_g):
    B, S, D = x.shape
    CT = ctx.shape[1]
    L = ada_w.shape[0]
    SA = S + CT
    assert S % ATT_TILE == 0 and CT == ROW_TILE and S % GRID_W == 0
    n_x_tiles = S // ROW_TILE
    n_tiles = SA // ROW_TILE
    H = DA_HEADS

    xy = jnp.concatenate([x, ctx], axis=1)
    cc = jnp.concatenate([c, c_ctx[None, :], jnp.zeros((16 - B - 1, D), F32)], axis=0)
    mod_all = _ada_all(cc, ada_w, ada_b).reshape(L, 16, 6, D)
    cos, sa, sb = _rope_tables(S, CT)

    qw = H * HEAD_W
    o_cq = 3 * qw
    o_kr = o_cq + MLA_Q_RANK + MLA_KV_RANK
    o_g = o_kr + MLA_ROPE
    wm = w_in[:, :, :o_kr].astype(BF16)
    wkr = jnp.pad(w_in[:, :, o_kr:o_g], ((0, 0), (0, 0), (0, HEAD_W - MLA_ROPE))).astype(BF16)
    wg = w_in[:, :, o_g:].astype(BF16)
    wqb4 = mla_wqb.reshape(L, MLA_Q_RANK, MLA_HEADS, MLA_NOPE + MLA_ROPE)
    wqb_n = wqb4[..., :MLA_NOPE].reshape(L, MLA_Q_RANK, MLA_HEADS * MLA_NOPE)
    wqb_r = jnp.pad(wqb4[..., MLA_NOPE:], ((0, 0), (0, 0), (0, 0), (0, HEAD_W - MLA_ROPE)))
    wqb = jnp.concatenate([wqb_n, wqb_r.reshape(L, MLA_Q_RANK, MLA_HEADS * HEAD_W)], axis=-1).astype(BF16)
    wkvb4 = mla_wkvb.reshape(L, MLA_KV_RANK, MLA_HEADS, MLA_NOPE + MLA_V)
    wkvb = jnp.concatenate([wkvb4[..., :MLA_NOPE].reshape(L, MLA_KV_RANK, -1),
                            wkvb4[..., MLA_NOPE:].reshape(L, MLA_KV_RANK, -1)], axis=-1).astype(BF16)
    wa = w_br_a.astype(BF16)
    wb = w_br_b.astype(BF16)
    wo = w_out.astype(BF16)
    RW = ROUTE_W
    rw = jnp.pad(router_w, ((0, 0), (0, 0), (0, RW - N_EXPERTS))).astype(BF16)
    rb = jnp.pad(router_b, ((0, 0), (0, RW - N_EXPERTS)), constant_values=NEG_BIG)
    wgu = moe_wgu.astype(BF16)
    wdn = moe_wdn.astype(BF16)
    NB = (B * SA * TOP_K + EXPERT_BLOCK - 1) // EXPERT_BLOCK + N_EXPERTS
    buf = jnp.zeros((NB * EXPERT_BLOCK, D // 2), jnp.uint32)

    for l in range(L):
        last = l == L - 1
        n_q = n_x_tiles if last else n_tiles
        mod = mod_all[l]
        lam_init = 0.8 - 0.6 * math.exp(-0.3 * l)
        lam = (jnp.exp(jnp.sum(da_lq1[l] * da_lk1[l])) - jnp.exp(jnp.sum(da_lq2[l] * da_lk2[l])) + lam_init)
        scal = jnp.stack([lam, jnp.asarray(1.0 - lam_init, F32)]).astype(F32)

        daq, dak, dav, mq, mk, mv, gates = _pre(
            xy, mod, norm1_g[l][None], cos, sa, sb, wm[l], wkr[l], wg[l],
            mla_qa_g[l][None], mla_kva_g[l][None], wqb[l], wkvb[l], n_x_tiles)
        oda, odac = _attention("da", daq, dak, dav, S, not last, scal=scal, g=da_subln_g[l][None])
        omla, omlac = _attention("mla", mq, mk, mv, S, not last)
        xy2, h2, route, counts = _merge(xy, oda, omla, odac, omlac, gates, mod, norm2_g[l][None], wa[l], wb[l], wo[l],
                                 rw[l], rb[l][None], n_x_tiles, n_q)

        SQ = n_q * ROW_TILE
        T = B * SQ
        pstart, block_e, n_used = _block_layout(counts, NB)
        route2 = route.reshape(T, ROUTE_W)
        e_idx = route2[:, 0:TOP_K].astype(jnp.int32)
        rank = route2[:, TOP_K:2 * TOP_K].astype(jnp.int32)
        base = jnp.sum(jnp.where(e_idx[:, :, None] == jnp.arange(N_EXPERTS, dtype=jnp.int32), pstart, 0), axis=-1)
        slot3 = (base + rank).reshape(T // ROW_TILE, 1, ROW_TILE * TOP_K)
        buf = _dispatch(slot3, h2.reshape(T, D // 2), buf)
        out_sorted = _experts(block_e, n_used, buf, wgu[l], moe_bgu[l], wdn[l], moe_bdn[l])
        xy = _combine(xy2.reshape(T, D), out_sorted, slot3, route2, mod, final_g[None],
                      n_x_tiles, n_q, B, last).reshape(B, SQ, D)
    return xy
```

```python
import functools
import math

import jax
import jax.numpy as jnp
from jax import lax
from jax.experimental import pallas as pl
from jax.experimental.pallas import tpu as pltpu

F32 = jnp.float32
BF16 = jnp.bfloat16

EPS = 1e-6
GRID_W = 64
ROPE_BASE = 10000.0
DA_HEADS = 4
DA_HD = 64
MLA_HEADS = 4
MLA_NOPE = 128
MLA_ROPE = 64
MLA_V = 128
MLA_Q_RANK = 384
MLA_KV_RANK = 256
N_EXPERTS = 32
TOP_K = 4
D_FF = 512
SWIGLU_LIMIT = 7.0
SWIGLU_ALPHA = 1.702
EXPERT_BLOCK = 512

LOG2E = math.log2(math.e)
HEAD_W = 128
ROW_TILE = 256
ATT_TILE = 512
KEY_CHUNK = 512
VMEM_LIMIT = 56 * 1024 * 1024


def _cparams(sem):
    return pltpu.CompilerParams(dimension_semantics=sem, vmem_limit_bytes=VMEM_LIMIT)


def _rms_rows(x):
    return x * lax.rsqrt(jnp.mean(x * x, axis=-1, keepdims=True) + EPS)


def _pack_halves(x):
    n = x.shape[1] // 2
    lo = lax.bitcast_convert_type(x[:, :n].astype(BF16).astype(F32), jnp.uint32)
    hi = lax.bitcast_convert_type(x[:, n:].astype(BF16).astype(F32), jnp.uint32)
    return lax.shift_right_logical(lo, jnp.uint32(16)) | hi


def _unpack_halves(w):
    lo = lax.bitcast_convert_type(lax.shift_left(w, jnp.uint32(16)), F32)
    hi = lax.bitcast_convert_type(w & jnp.uint32(0xFFFF0000), F32)
    return lo, hi


def _ada_kernel(c_ref, w_ref, b_ref, o_ref):
    c = c_ref[...]
    s = (c * jax.nn.sigmoid(c)).astype(BF16)
    o_ref[0] = jnp.dot(s, w_ref[0].astype(BF16), preferred_element_type=F32) + b_ref[0]


def _ada_all(cc, ada_w, ada_b):
    L, D, D6 = ada_w.shape
    R = cc.shape[0]
    nj = D6 // D
    return pl.pallas_call(
        _ada_kernel,
        out_shape=jax.ShapeDtypeStruct((L, R, D6), F32),
        grid=(L, nj),
        in_specs=[
            pl.BlockSpec((R, D), lambda l, j: (0, 0)),
            pl.BlockSpec((1, D, D), lambda l, j: (l, 0, j)),
            pl.BlockSpec((1, 1, D), lambda l, j: (l, 0, j)),
        ],
        out_specs=pl.BlockSpec((1, R, D), lambda l, j: (l, 0, j)),
        compiler_params=_cparams(("arbitrary", "arbitrary")),
        name="ada_mod",
    )(cc, ada_w, ada_b.reshape(L, 1, D6))


def _rope(z, cos, sin_a, sin_b):
    return z * cos + pltpu.roll(z, 112, 1) * sin_a + pltpu.roll(z, 16, 1) * sin_b


def _pre_kernel(x_ref, mod_ref, g_ref, cos_ref, sa_ref, sb_ref, wm_ref, wkr_ref, wg_ref,
                qag_ref, kvag_ref, wqb_ref, wkvb_ref,
                daq_ref, dak_ref, dav_ref, mq_ref, mk_ref, mv_ref, gate_ref, *, da_scale, mla_scale):
    x = x_ref[0]
    shift = mod_ref[0, 0:1, :]
    scale = mod_ref[0, 1:2, :]
    h = ((_rms_rows(x) * g_ref[...]) * (1.0 + scale) + shift).astype(BF16)
    cos = cos_ref[...]
    sa = sa_ref[...]
    sb = sb_ref[...]
    nh = DA_HEADS
    qw = nh * HEAD_W

    zq = jnp.dot(h, wm_ref[:, 0:qw], preferred_element_type=F32)
    for hd in range(nh):
        z = _rope(zq[:, hd * HEAD_W:(hd + 1) * HEAD_W], cos, sa, sb) * da_scale
        daq_ref[0, hd] = z.astype(BF16)
    zk = jnp.dot(h, wm_ref[:, qw:2 * qw], preferred_element_type=F32)
    for hd in range(nh):
        z = _rope(zk[:, hd * HEAD_W:(hd + 1) * HEAD_W], cos, sa, sb)
        dak_ref[0, hd] = z.astype(BF16)
    zv = jnp.dot(h, wm_ref[:, 2 * qw:3 * qw], preferred_element_type=F32)
    for hd in range(nh):
        dav_ref[0, hd] = zv[:, hd * HEAD_W:(hd + 1) * HEAD_W].astype(BF16)

    o0 = 3 * qw
    cq = jnp.dot(h, wm_ref[:, o0:o0 + MLA_Q_RANK], preferred_element_type=F32)
    cqn = (_rms_rows(cq) * qag_ref[...]).astype(BF16)
    qm = jnp.dot(cqn, wqb_ref[...], preferred_element_type=F32)
    for hd in range(MLA_HEADS):
        mq_ref[0, hd, :, 0:HEAD_W] = (qm[:, hd * HEAD_W:(hd + 1) * HEAD_W] * mla_scale).astype(BF16)
        zr = qm[:, (MLA_HEADS + hd) * HEAD_W:(MLA_HEADS + hd + 1) * HEAD_W]
        mq_ref[0, hd, :, HEAD_W:2 * HEAD_W] = (_rope(zr, cos, sa, sb) * mla_scale).astype(BF16)

    o1 = o0 + MLA_Q_RANK
    ckv = jnp.dot(h, wm_ref[:, o1:o1 + MLA_KV_RANK], preferred_element_type=F32)
    ckvn = (_rms_rows(ckv) * kvag_ref[...]).astype(BF16)
    kv = jnp.dot(ckvn, wkvb_ref[...], preferred_element_type=F32)
    zkr = jnp.dot(h, wkr_ref[...], preferred_element_type=F32)
    kr = _rope(zkr, cos, sa, sb).astype(BF16)
    for hd in range(MLA_HEADS):
        mk_ref[0, hd, :, 0:HEAD_W] = kv[:, hd * HEAD_W:(hd + 1) * HEAD_W].astype(BF16)
        mk_ref[0, hd, :, HEAD_W:2 * HEAD_W] = kr
        mv_ref[0, hd] = kv[:, (MLA_HEADS + hd) * HEAD_W:(MLA_HEADS + hd + 1) * HEAD_W].astype(BF16)

    gz = jnp.dot(h, wg_ref[...], preferred_element_type=F32)
    gate_ref[0] = jax.nn.sigmoid(gz).astype(BF16)


def _pre(xy, mod, g1, cos, sa, sb, wm, wkr, wg, qag, kvag, wqb, wkvb, n_x_tiles):
    B, SA, D = xy.shape
    nt = SA // ROW_TILE
    H = DA_HEADS
    const2 = lambda b, i: (0, 0)
    head_spec = lambda w: pl.BlockSpec((1, H, ROW_TILE, w), lambda b, i: (b, 0, i, 0))
    row_of = lambda b, i: jnp.where(i < n_x_tiles, b, B)
    kern = functools.partial(_pre_kernel, da_scale=LOG2E / math.sqrt(DA_HD),
                             mla_scale=LOG2E / math.sqrt(MLA_NOPE + MLA_ROPE))
    return pl.pallas_call(
        kern,
        out_shape=(
            jax.ShapeDtypeStruct((B, H, SA, HEAD_W), BF16),
            jax.ShapeDtypeStruct((B, H, SA, HEAD_W), BF16),
            jax.ShapeDtypeStruct((B, H, SA, HEAD_W), BF16),
            jax.ShapeDtypeStruct((B, H, SA, 2 * HEAD_W), BF16),
            jax.ShapeDtypeStruct((B, H, SA, 2 * HEAD_W), BF16),
            jax.ShapeDtypeStruct((B, H, SA, HEAD_W), BF16),
            jax.ShapeDtypeStruct((B, SA, 2 * D), BF16),
        ),
        grid=(B, nt),
        in_specs=[
            pl.BlockSpec((1, ROW_TILE, D), lambda b, i: (b, i, 0)),
            pl.BlockSpec((1, 6, D), lambda b, i: (row_of(b, i), 0, 0)),
            pl.BlockSpec((1, D), const2),
            pl.BlockSpec((ROW_TILE, HEAD_W), lambda b, i: (i, 0)),
            pl.BlockSpec((ROW_TILE, HEAD_W), lambda b, i: (i, 0)),
            pl.BlockSpec((ROW_TILE, HEAD_W), lambda b, i: (i, 0)),
            pl.BlockSpec(wm.shape, const2),
            pl.BlockSpec(wkr.shape, const2),
            pl.BlockSpec(wg.shape, const2),
            pl.BlockSpec(qag.shape, const2),
            pl.BlockSpec(kvag.shape, const2),
            pl.BlockSpec(wqb.shape, const2),
            pl.BlockSpec(wkvb.shape, const2),
        ],
        out_specs=(
            head_spec(HEAD_W), head_spec(HEAD_W), head_spec(HEAD_W),
            head_spec(2 * HEAD_W), head_spec(2 * HEAD_W), head_spec(HEAD_W),
            pl.BlockSpec((1, ROW_TILE, 2 * D), lambda b, i: (b, i, 0)),
        ),
        compiler_params=_cparams(("parallel", "arbitrary")),
        name="pre_mixer",
    )(xy, mod, g1, cos, sa, sb, wm, wkr, wg, qag, kvag, wqb, wkvb)


def _key_chunks(start, stop):
    out = []
    while start < stop:
        size = min(KEY_CHUNK, stop - start)
        out.append((start, size))
        start += size
    return out


def _fill_value_ext(v_ref, vx_ref):
    vx_ref[:, 0:HEAD_W] = v_ref[0, 0]
    vx_ref[:, HEAD_W:2 * HEAD_W] = jnp.ones((v_ref.shape[2], HEAD_W), BF16)


def _flash(streams, chunks):
    state = [(None, None)] * len(streams)
    for (st, sz) in chunks:
        for n, (qq, k_ref, vx_ref) in enumerate(streams):
            m, acc = state[n]
            s = lax.dot_general(qq, k_ref[0, 0, st:st + sz, :], (((1,), (1,)), ((), ())),
                                preferred_element_type=F32)
            mc = jnp.max(s, axis=-1, keepdims=True)
            m_new = mc if m is None else jnp.maximum(m, mc)
            e = jnp.exp2(s - m_new)
            pc = jnp.dot(e.astype(BF16), vx_ref[st:st + sz, :], preferred_element_type=F32)
            acc = pc if m is None else jnp.exp2(m - m_new) * acc + pc
            state[n] = (m_new, acc)
    return [acc[:, :HEAD_W] / acc[:, HEAD_W:HEAD_W + 1] for (_, acc) in state]


def _att_kernel(*refs, n_x_tiles, n_x_keys, n_keys, with_ctx):
    it = iter(refs)
    sc_ref = next(it)
    qd_ref = next(it)
    qdc_ref = next(it) if with_ctx else None
    kd_ref = next(it)
    vd_ref = next(it)
    g_ref = next(it)
    qm_ref = next(it)
    qmc_ref = next(it) if with_ctx else None
    km_ref = next(it)
    vm_ref = next(it)
    od_ref = next(it)
    odc_ref = next(it) if with_ctx else None
    om_ref = next(it)
    omc_ref = next(it) if with_ctx else None
    vxd_ref = next(it)
    vxm_ref = next(it)
    i = pl.program_id(2)

    @pl.when(i == 0)
    def _():
        _fill_value_ext(vd_ref, vxd_ref)
        _fill_value_ext(vm_ref, vxm_ref)

    def attend(qd, qm, chunks):
        lane = lax.broadcasted_iota(jnp.int32, qd.shape, 1)
        zero = jnp.zeros_like(qd)
        o1, o2 = _flash([(jnp.where(lane < DA_HD, qd, zero), kd_ref, vxd_ref),
                         (jnp.where(lane >= DA_HD, qd, zero), kd_ref, vxd_ref)], chunks)
        om, = _flash([(qm, km_ref, vxm_ref)], chunks)
        od = o1 - sc_ref[0] * o2
        od = (_rms_rows(od) * g_ref[...]) * sc_ref[1]
        return od.astype(BF16), om.astype(BF16)

    @pl.when(i < n_x_tiles)
    def _():
        od_ref[0], om_ref[0] = attend(qd_ref[0, 0], qm_ref[0, 0], _key_chunks(0, n_keys))

    if with_ctx:
        @pl.when(i >= n_x_tiles)
        def _():
            odc_ref[0], omc_ref[0] = attend(qdc_ref[0, 0], qmc_ref[0, 0], _key_chunks(n_x_keys, n_keys))


def _attention(qd, kd, vd, qm, km, vm, n_x_rows, with_ctx, scal, g):
    B, H, SA, _ = qd.shape
    n_ctx = SA - n_x_rows
    nxt = n_x_rows // ATT_TILE
    kw = dict(n_x_tiles=nxt, n_x_keys=n_x_rows, n_keys=SA, with_ctx=with_ctx)
    x_tile = lambda b, h, i: (b, h, jnp.minimum(i, nxt - 1), 0)
    c_tile = lambda b, h, i: (b, h, n_x_rows // n_ctx, 0)
    whole = lambda b, h, i: (b, h, 0, 0)
    in_specs = [pl.BlockSpec(memory_space=pltpu.SMEM)]
    args = [scal]
    for (q, k, v, extra) in ((qd, kd, vd, g), (qm, km, vm, None)):
        QW = q.shape[3]
        in_specs.append(pl.BlockSpec((1, 1, ATT_TILE, QW), x_tile))
        args.append(q)
        if with_ctx:
            in_specs.append(pl.BlockSpec((1, 1, n_ctx, QW), c_tile))
            args.append(q)
        in_specs += [pl.BlockSpec((1, 1, SA, QW), whole), pl.BlockSpec((1, 1, SA, HEAD_W), whole)]
        args += [k, v]
        if extra is not None:
            in_specs.append(pl.BlockSpec((1, HEAD_W), lambda b, h, i: (0, 0)))
            args.append(extra)
    out_shape, out_specs = [], []
    for _ in range(2):
        out_shape.append(jax.ShapeDtypeStruct((B, n_x_rows, H * HEAD_W), BF16))
        out_specs.append(pl.BlockSpec((1, ATT_TILE, HEAD_W), lambda b, h, i: (b, jnp.minimum(i, nxt - 1), h)))
        if with_ctx:
            out_shape.append(jax.ShapeDtypeStruct((B, n_ctx, H * HEAD_W), BF16))
            out_specs.append(pl.BlockSpec((1, n_ctx, HEAD_W), lambda b, h, i: (b, 0, h)))
    outs = pl.pallas_call(
        functools.partial(_att_kernel, **kw),
        out_shape=tuple(out_shape),
        grid=(B, H, nxt + (1 if with_ctx else 0)),
        in_specs=in_specs,
        out_specs=tuple(out_specs),
        scratch_shapes=[pltpu.VMEM((SA, 2 * HEAD_W), BF16), pltpu.VMEM((SA, 2 * HEAD_W), BF16)],
        compiler_params=_cparams(("parallel", "parallel", "arbitrary")),
        name="attention",
    )(*args)
    if with_ctx:
        return outs[0], outs[1], outs[2], outs[3]
    return outs[0], None, outs[1], None


ROUTE_W = 128
NEG_BIG = -3.0e38


def _merge_kernel(*refs, n_x_tiles, with_ctx):
    if with_ctx:
        (x_ref, oda_ref, omla_ref, odac_ref, omlac_ref, gate_ref, mod_ref, g2_ref, wa_ref, wb_ref, wo_ref,
         rw_ref, rb_ref, xo_ref, h2_ref, route_ref, cnt_ref, run_ref) = refs
        is_ctx = pl.program_id(1) >= n_x_tiles
        oda = jnp.where(is_ctx, odac_ref[0], oda_ref[0])
        omla = jnp.where(is_ctx, omlac_ref[0], omla_ref[0])
    else:
        (x_ref, oda_ref, omla_ref, gate_ref, mod_ref, g2_ref, wa_ref, wb_ref, wo_ref,
         rw_ref, rb_ref, xo_ref, h2_ref, route_ref, cnt_ref, run_ref) = refs
        oda = oda_ref[0]
        omla = omla_ref[0]
    D = x_ref.shape[2]
    tm = x_ref.shape[1]
    a = jnp.dot(oda, wa_ref[...], preferred_element_type=F32)
    b = jnp.dot(omla, wb_ref[...], preferred_element_type=F32)
    ga = gate_ref[0, :, 0:D].astype(F32)
    gb = gate_ref[0, :, D:2 * D].astype(F32)
    m = (ga * a + gb * b).astype(BF16)
    mix = jnp.dot(m, wo_ref[...], preferred_element_type=F32)
    xn = x_ref[0] + mod_ref[0, 2:3, :] * mix
    xo_ref[0] = xn
    h2 = (_rms_rows(xn) * g2_ref[...]) * (1.0 + mod_ref[0, 4:5, :]) + mod_ref[0, 3:4, :]
    h2_ref[0] = _pack_halves(h2)
    lg = jnp.dot(h2.astype(BF16), rw_ref[...], preferred_element_type=F32) + rb_ref[...]

    @pl.when((pl.program_id(0) == 0) & (pl.program_id(1) == 0))
    def _():
        run_ref[...] = jnp.zeros_like(run_ref)

    lane = lax.broadcasted_iota(jnp.int32, lg.shape, 1)
    lanef = lane.astype(F32)
    earlier = (lax.broadcasted_iota(jnp.int32, (tm, tm), 0) > lax.broadcasted_iota(jnp.int32, (tm, tm), 1))
    earlier = jnp.where(earlier, 1.0, 0.0).astype(BF16)
    run = run_ref[0:1, :]
    route = jnp.zeros(lg.shape, F32)
    vals = []
    for r in range(TOP_K):
        mx = jnp.max(lg, axis=-1, keepdims=True)
        idx = jnp.min(jnp.where(lg == mx, lanef, float(ROUTE_W)), axis=-1, keepdims=True)
        hit = lanef == idx
        hitf = jnp.where(hit, 1.0, 0.0)
        before = jnp.dot(earlier, hitf.astype(BF16), preferred_element_type=F32)
        rank = jnp.sum(jnp.where(hit, run + before, 0.0), axis=-1, keepdims=True)
        run = run + jnp.sum(hitf, axis=0, keepdims=True)
        route = jnp.where(lane == r, idx, route)
        route = jnp.where(lane == TOP_K + r, rank, route)
        vals.append(mx)
        lg = jnp.where(hit, NEG_BIG, lg)
    ex = [jnp.exp(v - vals[0]) for v in vals]
    den = ex[0] + ex[1] + ex[2] + ex[3]
    for r in range(TOP_K):
        route = jnp.where(lane == 2 * TOP_K + r, ex[r] / den, route)
    route_ref[0] = route
    run_ref[...] = jnp.broadcast_to(run, run_ref.shape)
    cnt_ref[...] = jnp.broadcast_to(run, cnt_ref.shape)


def _merge(xy, oda, omla, odac, omlac, gates, mod, g2, wa, wb, wo, rw, rb, n_x_tiles, n_q_tiles):
    B, SA, D = xy.shape
    with_ctx = odac is not None
    const2 = lambda b, i: (0, 0)
    row_of = lambda b, i: jnp.where(i < n_x_tiles, b, B)
    row_spec = lambda w: pl.BlockSpec((1, ROW_TILE, w), lambda b, i: (b, i, 0))
    x_rows = lambda w: pl.BlockSpec((1, ROW_TILE, w), lambda b, i: (b, jnp.minimum(i, n_x_tiles - 1), 0))
    c_rows = lambda w: pl.BlockSpec((1, ROW_TILE, w), lambda b, i: (b, 0, 0))
    AW = oda.shape[2]
    att_specs = [x_rows(AW), x_rows(AW)] + ([c_rows(AW), c_rows(AW)] if with_ctx else [])
    att_args = [oda, omla] + ([odac, omlac] if with_ctx else [])
    SQ = n_q_tiles * ROW_TILE
    return pl.pallas_call(
        functools.partial(_merge_kernel, n_x_tiles=n_x_tiles, with_ctx=with_ctx),
        out_shape=(
            jax.ShapeDtypeStruct((B, SQ, D), F32),
            jax.ShapeDtypeStruct((B, SQ, D // 2), jnp.uint32),
            jax.ShapeDtypeStruct((B, SQ, ROUTE_W), F32),
            jax.ShapeDtypeStruct((8, ROUTE_W), F32),
        ),
        grid=(B, n_q_tiles),
        in_specs=[
            row_spec(D), *att_specs, row_spec(2 * D),
            pl.BlockSpec((1, 6, D), lambda b, i: (row_of(b, i), 0, 0)),
            pl.BlockSpec((1, D), const2),
            pl.BlockSpec(wa.shape, const2), pl.BlockSpec(wb.shape, const2), pl.BlockSpec(wo.shape, const2),
            pl.BlockSpec(rw.shape, const2), pl.BlockSpec(rb.shape, const2),
        ],
        out_specs=(row_spec(D), row_spec(D // 2), row_spec(ROUTE_W), pl.BlockSpec((8, ROUTE_W), const2)),
        scratch_shapes=[pltpu.VMEM((8, ROUTE_W), F32)],
        compiler_params=_cparams(("arbitrary", "arbitrary")),
        name="merge_post",
    )(xy, *att_args, gates, mod, g2, wa, wb, wo, rw, rb)


def _row_copy(src_ref, src_row, dst_ref, dst_row, sem):
    return pltpu.make_async_copy(src_ref.at[pl.ds(src_row, 1)], dst_ref.at[pl.ds(dst_row, 1)], sem)


def _dispatch_kernel(slot_ref, h_ref, buf_in_ref, buf_ref, sem):
    del buf_in_ref
    tm = h_ref.shape[0]

    for r in range(tm):
        for k in range(TOP_K):
            _row_copy(h_ref, r, buf_ref, slot_ref[0, 0, r * TOP_K + k], sem).start(priority=k % 2)
    for k in range(TOP_K):
        pltpu.make_async_copy(h_ref, buf_ref.at[pl.ds(0, tm)], sem).wait()


def _dispatch(slot3, h2, buf):
    T, D = h2.shape
    nt = T // ROW_TILE
    return pl.pallas_call(
        _dispatch_kernel,
        out_shape=jax.ShapeDtypeStruct(buf.shape, buf.dtype),
        grid=(nt,),
        in_specs=[
            pl.BlockSpec((1, 1, ROW_TILE * TOP_K), lambda t: (t, 0, 0), memory_space=pltpu.SMEM),
            pl.BlockSpec((ROW_TILE, D), lambda t: (t, 0)),
            pl.BlockSpec(memory_space=pl.ANY),
        ],
        out_specs=pl.BlockSpec(memory_space=pl.ANY),
        scratch_shapes=[pltpu.SemaphoreType.DMA(())],
        input_output_aliases={2: 0},
        compiler_params=_cparams(("arbitrary",)),
        name="moe_dispatch",
    )(slot3, h2, buf)


def _expert_kernel(be_ref, nu_ref, x_ref, wgu_ref, bgu_ref, wdn_ref, bdn_ref, o_ref):
    n = pl.program_id(0)

    @pl.when(n < nu_ref[0])
    def _():
        xa, xb = _unpack_halves(x_ref[...])
        half = xa.shape[1]
        h = (jnp.dot(xa.astype(BF16), wgu_ref[0, 0:half, :], preferred_element_type=F32)
             + jnp.dot(xb.astype(BF16), wgu_ref[0, half:2 * half, :], preferred_element_type=F32) + bgu_ref[0])
        gate = jnp.minimum(h[:, :D_FF], SWIGLU_LIMIT)
        lin = jnp.clip(h[:, D_FF:], -SWIGLU_LIMIT, SWIGLU_LIMIT)
        act = (lin + 1.0) * gate * jax.nn.sigmoid(SWIGLU_ALPHA * gate)
        o_ref[...] = _pack_halves(jnp.dot(act.astype(BF16), wdn_ref[0], preferred_element_type=F32) + bdn_ref[0])

    @pl.when(n >= nu_ref[0])
    def _():
        o_ref[...] = jnp.zeros_like(o_ref)


def _experts(block_e, n_used, buf, wgu, bgu, wdn, bdn):
    R, DP = buf.shape
    D = 2 * DP
    NB = R // EXPERT_BLOCK
    E = wgu.shape[0]
    return pl.pallas_call(
        _expert_kernel,
        out_shape=jax.ShapeDtypeStruct((R, DP), jnp.uint32),
        grid_spec=pltpu.PrefetchScalarGridSpec(
            num_scalar_prefetch=2,
            grid=(NB,),
            in_specs=[
                pl.BlockSpec((EXPERT_BLOCK, DP), lambda n, be, nu: (jnp.minimum(n, nu[0] - 1), 0)),
                pl.BlockSpec((1, D, 2 * D_FF), lambda n, be, nu: (be[n], 0, 0)),
                pl.BlockSpec((1, 1, 2 * D_FF), lambda n, be, nu: (be[n], 0, 0)),
                pl.BlockSpec((1, D_FF, D), lambda n, be, nu: (be[n], 0, 0)),
                pl.BlockSpec((1, 1, D), lambda n, be, nu: (be[n], 0, 0)),
            ],
            out_specs=pl.BlockSpec((EXPERT_BLOCK, DP), lambda n, be, nu: (n, 0)),
        ),
        compiler_params=_cparams(("arbitrary",)),
        name="expert_mlp",
    )(block_e, n_used, buf, wgu, bgu.reshape(E, 1, 2 * D_FF), wdn, bdn.reshape(E, 1, D))


def _combine_kernel(slot_ref, nslot_ref, x_ref, route_ref, mod_ref, fg_ref, y_hbm, o_ref, ybuf, sems, *, final):
    t = pl.program_id(0)
    nt = pl.num_programs(0)
    tm = x_ref.shape[0]
    cur = lax.rem(t, 2)

    def gather(s_ref, half):
        for r in range(tm):
            for k in range(TOP_K):
                _row_copy(y_hbm, s_ref[0, 0, r * TOP_K + k], ybuf.at[half, k], r,
                          sems.at[half]).start(priority=k % 2)

    @pl.when(t == 0)
    def _():
        gather(slot_ref, 0)

    @pl.when(t + 1 < nt)
    def _():
        gather(nslot_ref, 1 - cur)

    for k in range(TOP_K):
        pltpu.make_async_copy(y_hbm.at[pl.ds(0, tm)], ybuf.at[cur, k], sems.at[cur]).wait()

    w = route_ref[:, 2 * TOP_K:3 * TOP_K]
    f_lo = f_hi = None
    for k in range(TOP_K):
        y_lo, y_hi = _unpack_halves(ybuf[cur, k])
        f_lo = w[:, k:k + 1] * y_lo if k == 0 else f_lo + w[:, k:k + 1] * y_lo
        f_hi = w[:, k:k + 1] * y_hi if k == 0 else f_hi + w[:, k:k + 1] * y_hi
    f = jnp.concatenate([f_lo, f_hi], axis=-1)
    xn = x_ref[...] + mod_ref[0, 5:6, :] * f
    if final:
        xn = _rms_rows(xn) * fg_ref[...]
    o_ref[...] = xn


def _combine(x2, y_sorted, slot3, route, mod, fg, n_x_tiles, n_q_tiles, n_batch, final):
    T, D = x2.shape
    nt = T // ROW_TILE
    row_of = lambda t: jnp.where(t % n_q_tiles < n_x_tiles, t // n_q_tiles, n_batch)
    slot_block = (1, 1, ROW_TILE * TOP_K)
    return pl.pallas_call(
        functools.partial(_combine_kernel, final=final),
        out_shape=jax.ShapeDtypeStruct((T, D), F32),
        grid=(nt,),
        in_specs=[
            pl.BlockSpec(slot_block, lambda t: (t, 0, 0), memory_space=pltpu.SMEM),
            pl.BlockSpec(slot_block, lambda t: (jnp.minimum(t + 1, nt - 1), 0, 0), memory_space=pltpu.SMEM),
            pl.BlockSpec((ROW_TILE, D), lambda t: (t, 0)),
            pl.BlockSpec((ROW_TILE, ROUTE_W), lambda t: (t, 0)),
            pl.BlockSpec((1, 6, D), lambda t: (row_of(t), 0, 0)),
            pl.BlockSpec((1, D), lambda t: (0, 0)),
            pl.BlockSpec(memory_space=pl.ANY),
        ],
        out_specs=pl.BlockSpec((ROW_TILE, D), lambda t: (t, 0)),
        scratch_shapes=[pltpu.VMEM((2, TOP_K, ROW_TILE, D // 2), jnp.uint32), pltpu.SemaphoreType.DMA((2,))],
        compiler_params=_cparams(("arbitrary",)),
        name="moe_combine",
    )(slot3, slot3, x2, route, mod, fg, y_sorted)


def _block_layout(counts, n_blocks):
    sizes = counts[0, :N_EXPERTS].astype(jnp.int32)
    padded = ((sizes + EXPERT_BLOCK - 1) // EXPERT_BLOCK) * EXPERT_BLOCK
    pend = jnp.cumsum(padded)
    pstart = pend - padded
    first_row = jnp.arange(n_blocks, dtype=jnp.int32) * EXPERT_BLOCK
    block_e = jnp.minimum(jnp.sum(pend[None, :] <= first_row[:, None], axis=1), N_EXPERTS - 1).astype(jnp.int32)
    n_used = (pend[-1] // EXPERT_BLOCK).astype(jnp.int32).reshape(1)
    return pstart, block_e, n_used


def _rope_tables(S, n_ctx):
    rows = jnp.repeat(jnp.arange(S // GRID_W, dtype=F32), GRID_W)
    cols = jnp.tile(jnp.arange(GRID_W, dtype=F32), S // GRID_W)
    half = DA_HD // 2
    freqs = ROPE_BASE ** (-jnp.arange(0, half, 2, dtype=F32) / half)
    ar = rows[:, None] * freqs
    ac = cols[:, None] * freqs
    ang = jnp.concatenate([ar, ar, ac, ac], axis=-1)
    cos = jnp.concatenate([jnp.cos(ang), jnp.ones((n_ctx, DA_HD), F32)], axis=0)
    sin = jnp.concatenate([jnp.sin(ang), jnp.zeros((n_ctx, DA_HD), F32)], axis=0)
    cos = jnp.tile(cos, (1, HEAD_W // DA_HD))
    sin = jnp.tile(sin, (1, HEAD_W // DA_HD))
    lo = (jnp.arange(HEAD_W) % (DA_HD // 2)) < (DA_HD // 4)
    return cos, jnp.where(lo, -sin, 0.0), jnp.where(lo, 0.0, sin)


def kernel(x, c, ctx, c_ctx, ada_w, ada_b, norm1_g, norm2_g, w_in, da_lq1, da_lk1, da_lq2, da_lk2, da_subln_g, mla_qa_g, mla_kva_g, mla_wqb, mla_wkvb, w_br_a, w_br_b, w_out, router_w, router_b, moe_wgu, moe_bgu, moe_wdn, moe_bdn, final_g):
    B, S, D = x.shape
    CT = ctx.shape[1]
    L = ada_w.shape[0]
    SA = S + CT
    assert S % ATT_TILE == 0 and CT == ROW_TILE and S % GRID_W == 0
    n_x_tiles = S // ROW_TILE
    n_tiles = SA // ROW_TILE
    H = DA_HEADS

    xy = jnp.concatenate([x, ctx], axis=1)
    cc = jnp.concatenate([c, c_ctx[None, :], jnp.zeros((16 - B - 1, D), F32)], axis=0)
    mod_all = _ada_all(cc, ada_w, ada_b).reshape(L, 16, 6, D)
    cos, sa, sb = _rope_tables(S, CT)

    qw = H * HEAD_W
    o_cq = 3 * qw
    o_kr = o_cq + MLA_Q_RANK + MLA_KV_RANK
    o_g = o_kr + MLA_ROPE
    wm = w_in[:, :, :o_kr].astype(BF16)
    wkr = jnp.pad(w_in[:, :, o_kr:o_g], ((0, 0), (0, 0), (0, HEAD_W - MLA_ROPE))).astype(BF16)
    wg = w_in[:, :, o_g:].astype(BF16)
    wqb4 = mla_wqb.reshape(L, MLA_Q_RANK, MLA_HEADS, MLA_NOPE + MLA_ROPE)
    wqb_n = wqb4[..., :MLA_NOPE].reshape(L, MLA_Q_RANK, MLA_HEADS * MLA_NOPE)
    wqb_r = jnp.pad(wqb4[..., MLA_NOPE:], ((0, 0), (0, 0), (0, 0), (0, HEAD_W - MLA_ROPE)))
    wqb = jnp.concatenate([wqb_n, wqb_r.reshape(L, MLA_Q_RANK, MLA_HEADS * HEAD_W)], axis=-1).astype(BF16)
    wkvb4 = mla_wkvb.reshape(L, MLA_KV_RANK, MLA_HEADS, MLA_NOPE + MLA_V)
    wkvb = jnp.concatenate([wkvb4[..., :MLA_NOPE].reshape(L, MLA_KV_RANK, -1),
                            wkvb4[..., MLA_NOPE:].reshape(L, MLA_KV_RANK, -1)], axis=-1).astype(BF16)
    wa = w_br_a.astype(BF16)
    wb = w_br_b.astype(BF16)
    wo = w_out.astype(BF16)
    RW = ROUTE_W
    rw = jnp.pad(router_w, ((0, 0), (0, 0), (0, RW - N_EXPERTS))).astype(BF16)
    rb = jnp.pad(router_b, ((0, 0), (0, RW - N_EXPERTS)), constant_values=NEG_BIG)
    wgu = moe_wgu.astype(BF16)
    wdn = moe_wdn.astype(BF16)
    NB = (B * SA * TOP_K + EXPERT_BLOCK - 1) // EXPERT_BLOCK + N_EXPERTS
    buf = jnp.zeros((NB * EXPERT_BLOCK, D // 2), jnp.uint32)

    for l in range(L):
        last = l == L - 1
        n_q = n_x_tiles if last else n_tiles
        mod = mod_all[l]
        lam_init = 0.8 - 0.6 * math.exp(-0.3 * l)
        lam = (jnp.exp(jnp.sum(da_lq1[l] * da_lk1[l])) - jnp.exp(jnp.sum(da_lq2[l] * da_lk2[l])) + lam_init)
        scal = jnp.stack([lam, jnp.asarray(1.0 - lam_init, F32)]).astype(F32)

        daq, dak, dav, mq, mk, mv, gates = _pre(
            xy, mod, norm1_g[l][None], cos, sa, sb, wm[l], wkr[l], wg[l],
            mla_qa_g[l][None], mla_kva_g[l][None], wqb[l], wkvb[l], n_x_tiles)
        oda, odac, omla, omlac = _attention(daq, dak, dav, mq, mk, mv, S, not last, scal, da_subln_g[l][None])
        xy2, h2, route, counts = _merge(xy, oda, omla, odac, omlac, gates, mod, norm2_g[l][None], wa[l], wb[l], wo[l],
                                 rw[l], rb[l][None], n_x_tiles, n_q)

        SQ = n_q * ROW_TILE
        T = B * SQ
        pstart, block_e, n_used = _block_layout(counts, NB)
        route2 = route.reshape(T, ROUTE_W)
        e_idx = route2[:, 0:TOP_K].astype(jnp.int32)
        rank = route2[:, TOP_K:2 * TOP_K].astype(jnp.int32)
        base = jnp.sum(jnp.where(e_idx[:, :, None] == jnp.arange(N_EXPERTS, dtype=jnp.int32), pstart, 0), axis=-1)
        slot3 = (base + rank).reshape(T // ROW_TILE, 1, ROW_TILE * TOP_K)
        buf = _dispatch(slot3, h2.reshape(T, D // 2), buf)
        out_sorted = _experts(block_e, n_used, buf, wgu[l], moe_bgu[l], wdn[l], moe_bdn[l])
        xy = _combine(xy2.reshape(T, D), out_sorted, slot3, route2, mod, final_g[None],
                      n_x_tiles, n_q, B, last).reshape(B, SQ, D)
    return xy
```

```python
import functools
import math

import jax
import jax.numpy as jnp
from jax import lax
from jax.experimental import pallas as pl
from jax.experimental.pallas import tpu as pltpu

F32 = jnp.float32
BF16 = jnp.bfloat16

EPS = 1e-6
GRID_W = 64
ROPE_BASE = 10000.0
DA_HEADS = 4
DA_HD = 64
MLA_HEADS = 4
MLA_NOPE = 128
MLA_ROPE = 64
MLA_V = 128
MLA_Q_RANK = 384
MLA_KV_RANK = 256
N_EXPERTS = 32
TOP_K = 4
D_FF = 512
SWIGLU_LIMIT = 7.0
SWIGLU_ALPHA = 1.702
EXPERT_BLOCK = 512

LOG2E = math.log2(math.e)
HEAD_W = 128
ROW_TILE = 256
ATT_TILE = 1024
MOVE_TILE = 512
KEY_CHUNK = 512
VMEM_LIMIT = 56 * 1024 * 1024


def _cparams(sem):
    return pltpu.CompilerParams(dimension_semantics=sem, vmem_limit_bytes=VMEM_LIMIT)


def _rms_rows(x):
    return x * lax.rsqrt(jnp.mean(x * x, axis=-1, keepdims=True) + EPS)


def _pack_halves(x):
    n = x.shape[1] // 2
    lo = lax.bitcast_convert_type(x[:, :n].astype(BF16).astype(F32), jnp.uint32)
    hi = lax.bitcast_convert_type(x[:, n:].astype(BF16).astype(F32), jnp.uint32)
    return lax.shift_right_logical(lo, jnp.uint32(16)) | hi


def _unpack_halves(w):
    lo = lax.bitcast_convert_type(lax.shift_left(w, jnp.uint32(16)), F32)
    hi = lax.bitcast_convert_type(w & jnp.uint32(0xFFFF0000), F32)
    return lo, hi


def _ada_kernel(c_ref, w_ref, b_ref, o_ref):
    c = c_ref[...]
    s = (c * jax.nn.sigmoid(c)).astype(BF16)
    o_ref[0] = jnp.dot(s, w_ref[0].astype(BF16), preferred_element_type=F32) + b_ref[0]


def _ada_all(cc, ada_w, ada_b):
    L, D, D6 = ada_w.shape
    R = cc.shape[0]
    nj = D6 // D
    return pl.pallas_call(
        _ada_kernel,
        out_shape=jax.ShapeDtypeStruct((L, R, D6), F32),
        grid=(L, nj),
        in_specs=[
            pl.BlockSpec((R, D), lambda l, j: (0, 0)),
            pl.BlockSpec((1, D, D), lambda l, j: (l, 0, j)),
            pl.BlockSpec((1, 1, D), lambda l, j: (l, 0, j)),
        ],
        out_specs=pl.BlockSpec((1, R, D), lambda l, j: (l, 0, j)),
        compiler_params=_cparams(("arbitrary", "arbitrary")),
        name="ada_mod",
    )(cc, ada_w, ada_b.reshape(L, 1, D6))


def _rope(z, cos, sin_a, sin_b):
    return z * cos + pltpu.roll(z, 112, 1) * sin_a + pltpu.roll(z, 16, 1) * sin_b


def _pre_kernel(x_ref, mod_ref, g_ref, cos_ref, sa_ref, sb_ref, wm_ref, wkr_ref, wg_ref,
                qag_ref, kvag_ref, wqb_ref, wkvb_ref,
                daq_ref, dak_ref, dav_ref, mq_ref, mk_ref, mv_ref, gate_ref, *, da_scale, mla_scale):
    x = x_ref[0]
    shift = mod_ref[0, 0:1, :]
    scale = mod_ref[0, 1:2, :]
    h = ((_rms_rows(x) * g_ref[...]) * (1.0 + scale) + shift).astype(BF16)
    cos = cos_ref[...]
    sa = sa_ref[...]
    sb = sb_ref[...]
    nh = DA_HEADS
    qw = nh * HEAD_W

    zq = jnp.dot(h, wm_ref[:, 0:qw], preferred_element_type=F32)
    for hd in range(nh):
        z = _rope(zq[:, hd * HEAD_W:(hd + 1) * HEAD_W], cos, sa, sb) * da_scale
        daq_ref[0, hd] = z.astype(BF16)
    zk = jnp.dot(h, wm_ref[:, qw:2 * qw], preferred_element_type=F32)
    for hd in range(nh):
        z = _rope(zk[:, hd * HEAD_W:(hd + 1) * HEAD_W], cos, sa, sb)
        dak_ref[0, hd] = z.astype(BF16)
    zv = jnp.dot(h, wm_ref[:, 2 * qw:3 * qw], preferred_element_type=F32)
    for hd in range(nh):
        dav_ref[0, hd] = zv[:, hd * HEAD_W:(hd + 1) * HEAD_W].astype(BF16)

    o0 = 3 * qw
    cq = jnp.dot(h, wm_ref[:, o0:o0 + MLA_Q_RANK], preferred_element_type=F32)
    cqn = (_rms_rows(cq) * qag_ref[...]).astype(BF16)
    qm = jnp.dot(cqn, wqb_ref[...], preferred_element_type=F32)
    for hd in range(MLA_HEADS):
        mq_ref[0, hd, :, 0:HEAD_W] = (qm[:, hd * HEAD_W:(hd + 1) * HEAD_W] * mla_scale).astype(BF16)
        zr = qm[:, (MLA_HEADS + hd) * HEAD_W:(MLA_HEADS + hd + 1) * HEAD_W]
        mq_ref[0, hd, :, HEAD_W:2 * HEAD_W] = (_rope(zr, cos, sa, sb) * mla_scale).astype(BF16)

    o1 = o0 + MLA_Q_RANK
    ckv = jnp.dot(h, wm_ref[:, o1:o1 + MLA_KV_RANK], preferred_element_type=F32)
    ckvn = (_rms_rows(ckv) * kvag_ref[...]).astype(BF16)
    kv = jnp.dot(ckvn, wkvb_ref[...], preferred_element_type=F32)
    zkr = jnp.dot(h, wkr_ref[...], preferred_element_type=F32)
    kr = _rope(zkr, cos, sa, sb).astype(BF16)
    for hd in range(MLA_HEADS):
        mk_ref[0, hd, :, 0:HEAD_W] = kv[:, hd * HEAD_W:(hd + 1) * HEAD_W].astype(BF16)
        mk_ref[0, hd, :, HEAD_W:2 * HEAD_W] = kr
        mv_ref[0, hd] = kv[:, (MLA_HEADS + hd) * HEAD_W:(MLA_HEADS + hd + 1) * HEAD_W].astype(BF16)

    gz = jnp.dot(h, wg_ref[...], preferred_element_type=F32)
    gate_ref[0] = jax.nn.sigmoid(gz).astype(BF16)


def _pre(xy, mod, g1, cos, sa, sb, wm, wkr, wg, qag, kvag, wqb, wkvb, n_x_tiles):
    B, SA, D = xy.shape
    nt = SA // ROW_TILE
    H = DA_HEADS
    const2 = lambda b, i: (0, 0)
    head_spec = lambda w: pl.BlockSpec((1, H, ROW_TILE, w), lambda b, i: (b, 0, i, 0))
    row_of = lambda b, i: jnp.where(i < n_x_tiles, b, B)
    kern = functools.partial(_pre_kernel, da_scale=LOG2E / math.sqrt(DA_HD),
                             mla_scale=LOG2E / math.sqrt(MLA_NOPE + MLA_ROPE))
    return pl.pallas_call(
        kern,
        out_shape=(
            jax.ShapeDtypeStruct((B, H, SA, HEAD_W), BF16),
            jax.ShapeDtypeStruct((B, H, SA, HEAD_W), BF16),
            jax.ShapeDtypeStruct((B, H, SA, HEAD_W), BF16),
            jax.ShapeDtypeStruct((B, H, SA, 2 * HEAD_W), BF16),
            jax.ShapeDtypeStruct((B, H, SA, 2 * HEAD_W), BF16),
            jax.ShapeDtypeStruct((B, H, SA, HEAD_W), BF16),
            jax.ShapeDtypeStruct((B, SA, 2 * D), BF16),
        ),
        grid=(B, nt),
        in_specs=[
            pl.BlockSpec((1, ROW_TILE, D), lambda b, i: (b, i, 0)),
            pl.BlockSpec((1, 6, D), lambda b, i: (row_of(b, i), 0, 0)),
            pl.BlockSpec((1, D), const2),
            pl.BlockSpec((ROW_TILE, HEAD_W), lambda b, i: (i, 0)),
            pl.BlockSpec((ROW_TILE, HEAD_W), lambda b, i: (i, 0)),
            pl.BlockSpec((ROW_TILE, HEAD_W), lambda b, i: (i, 0)),
            pl.BlockSpec(wm.shape, const2),
            pl.BlockSpec(wkr.shape, const2),
            pl.BlockSpec(wg.shape, const2),
            pl.BlockSpec(qag.shape, const2),
            pl.BlockSpec(kvag.shape, const2),
            pl.BlockSpec(wqb.shape, const2),
            pl.BlockSpec(wkvb.shape, const2),
        ],
        out_specs=(
            head_spec(HEAD_W), head_spec(HEAD_W), head_spec(HEAD_W),
            head_spec(2 * HEAD_W), head_spec(2 * HEAD_W), head_spec(HEAD_W),
            pl.BlockSpec((1, ROW_TILE, 2 * D), lambda b, i: (b, i, 0)),
        ),
        compiler_params=_cparams(("parallel", "arbitrary")),
        name="pre_mixer",
    )(xy, mod, g1, cos, sa, sb, wm, wkr, wg, qag, kvag, wqb, wkvb)


def _key_chunks(start, stop):
    out = []
    while start < stop:
        size = min(KEY_CHUNK, stop - start)
        out.append((start, size))
        start += size
    return out


def _fill_value_ext(v_ref, vx_ref):
    vx_ref[:, 0:HEAD_W] = v_ref[0, 0]
    vx_ref[:, HEAD_W:2 * HEAD_W] = jnp.ones((v_ref.shape[2], HEAD_W), BF16)


def _flash(streams, chunks):
    state = [(None, None)] * len(streams)
    for (st, sz) in chunks:
        for n, (qq, k_ref, vx_ref) in enumerate(streams):
            m, acc = state[n]
            s = lax.dot_general(qq, k_ref[0, 0, st:st + sz, :], (((1,), (1,)), ((), ())),
                                preferred_element_type=F32)
            mc = jnp.max(s, axis=-1, keepdims=True)
            m_new = mc if m is None else jnp.maximum(m, mc)
            e = jnp.exp2(s - m_new)
            pc = jnp.dot(e.astype(BF16), vx_ref[st:st + sz, :], preferred_element_type=F32)
            acc = pc if m is None else jnp.exp2(m - m_new) * acc + pc
            state[n] = (m_new, acc)
    return [acc[:, :HEAD_W] / acc[:, HEAD_W:HEAD_W + 1] for (_, acc) in state]


def _att_kernel(*refs, n_x_tiles, n_x_keys, n_keys, with_ctx):
    it = iter(refs)
    sc_ref = next(it)
    qd_ref = next(it)
    qdc_ref = next(it) if with_ctx else None
    kd_ref = next(it)
    vd_ref = next(it)
    g_ref = next(it)
    qm_ref = next(it)
    qmc_ref = next(it) if with_ctx else None
    km_ref = next(it)
    vm_ref = next(it)
    od_ref = next(it)
    odc_ref = next(it) if with_ctx else None
    om_ref = next(it)
    omc_ref = next(it) if with_ctx else None
    vxd_ref = next(it)
    vxm_ref = next(it)
    i = pl.program_id(2)

    @pl.when(i == 0)
    def _():
        _fill_value_ext(vd_ref, vxd_ref)
        _fill_value_ext(vm_ref, vxm_ref)

    def attend(qd, qm, chunks):
        lane = lax.broadcasted_iota(jnp.int32, qd.shape, 1)
        zero = jnp.zeros_like(qd)
        o1, o2 = _flash([(jnp.where(lane < DA_HD, qd, zero), kd_ref, vxd_ref),
                         (jnp.where(lane >= DA_HD, qd, zero), kd_ref, vxd_ref)], chunks)
        om, = _flash([(qm, km_ref, vxm_ref)], chunks)
        od = o1 - sc_ref[0] * o2
        od = (_rms_rows(od) * g_ref[...]) * sc_ref[1]
        return od.astype(BF16), om.astype(BF16)

    @pl.when(i < n_x_tiles)
    def _():
        od_ref[0], om_ref[0] = attend(qd_ref[0, 0], qm_ref[0, 0], _key_chunks(0, n_keys))

    if with_ctx:
        @pl.when(i >= n_x_tiles)
        def _():
            odc_ref[0], omc_ref[0] = attend(qdc_ref[0, 0], qmc_ref[0, 0], _key_chunks(n_x_keys, n_keys))


def _attention(qd, kd, vd, qm, km, vm, n_x_rows, with_ctx, scal, g):
    B, H, SA, _ = qd.shape
    n_ctx = SA - n_x_rows
    nxt = n_x_rows // ATT_TILE
    kw = dict(n_x_tiles=nxt, n_x_keys=n_x_rows, n_keys=SA, with_ctx=with_ctx)
    x_tile = lambda b, h, i: (b, h, jnp.minimum(i, nxt - 1), 0)
    c_tile = lambda b, h, i: (b, h, n_x_rows // n_ctx, 0)
    whole = lambda b, h, i: (b, h, 0, 0)
    in_specs = [pl.BlockSpec(memory_space=pltpu.SMEM)]
    args = [scal]
    for (q, k, v, extra) in ((qd, kd, vd, g), (qm, km, vm, None)):
        QW = q.shape[3]
        in_specs.append(pl.BlockSpec((1, 1, ATT_TILE, QW), x_tile))
        args.append(q)
        if with_ctx:
            in_specs.append(pl.BlockSpec((1, 1, n_ctx, QW), c_tile))
            args.append(q)
        in_specs += [pl.BlockSpec((1, 1, SA, QW), whole), pl.BlockSpec((1, 1, SA, HEAD_W), whole)]
        args += [k, v]
        if extra is not None:
            in_specs.append(pl.BlockSpec((1, HEAD_W), lambda b, h, i: (0, 0)))
            args.append(extra)
    out_shape, out_specs = [], []
    for _ in range(2):
        out_shape.append(jax.ShapeDtypeStruct((B, n_x_rows, H * HEAD_W), BF16))
        out_specs.append(pl.BlockSpec((1, ATT_TILE, HEAD_W), lambda b, h, i: (b, jnp.minimum(i, nxt - 1), h)))
        if with_ctx:
            out_shape.append(jax.ShapeDtypeStruct((B, n_ctx, H * HEAD_W), BF16))
            out_specs.append(pl.BlockSpec((1, n_ctx, HEAD_W), lambda b, h, i: (b, 0, h)))
    outs = pl.pallas_call(
        functools.partial(_att_kernel, **kw),
        out_shape=tuple(out_shape),
        grid=(B, H, nxt + (1 if with_ctx else 0)),
        in_specs=in_specs,
        out_specs=tuple(out_specs),
        scratch_shapes=[pltpu.VMEM((SA, 2 * HEAD_W), BF16), pltpu.VMEM((SA, 2 * HEAD_W), BF16)],
        compiler_params=_cparams(("parallel", "parallel", "arbitrary")),
        name="attention",
    )(*args)
    if with_ctx:
        return outs[0], outs[1], outs[2], outs[3]
    return outs[0], None, outs[1], None


ROUTE_W = 128
NEG_BIG = -3.0e38


def _merge_kernel(*refs, n_x_tiles, with_ctx):
    if with_ctx:
        (x_ref, oda_ref, omla_ref, odac_ref, omlac_ref, gate_ref, mod_ref, g2_ref, wa_ref, wb_ref, wo_ref,
         rw_ref, rb_ref, xo_ref, h2_ref, route_ref, cnt_ref, run_ref) = refs
        is_ctx = pl.program_id(1) >= n_x_tiles
        oda = jnp.where(is_ctx, odac_ref[0], oda_ref[0])
        omla = jnp.where(is_ctx, omlac_ref[0], omla_ref[0])
    else:
        (x_ref, oda_ref, omla_ref, gate_ref, mod_ref, g2_ref, wa_ref, wb_ref, wo_ref,
         rw_ref, rb_ref, xo_ref, h2_ref, route_ref, cnt_ref, run_ref) = refs
        oda = oda_ref[0]
        omla = omla_ref[0]
    D = x_ref.shape[2]
    tm = x_ref.shape[1]
    a = jnp.dot(oda, wa_ref[...], preferred_element_type=F32)
    b = jnp.dot(omla, wb_ref[...], preferred_element_type=F32)
    ga = gate_ref[0, :, 0:D].astype(F32)
    gb = gate_ref[0, :, D:2 * D].astype(F32)
    m = (ga * a + gb * b).astype(BF16)
    mix = jnp.dot(m, wo_ref[...], preferred_element_type=F32)
    xn = x_ref[0] + mod_ref[0, 2:3, :] * mix
    xo_ref[0] = xn
    h2 = (_rms_rows(xn) * g2_ref[...]) * (1.0 + mod_ref[0, 4:5, :]) + mod_ref[0, 3:4, :]
    h2_ref[0] = _pack_halves(h2)
    lg = jnp.dot(h2.astype(BF16), rw_ref[...], preferred_element_type=F32) + rb_ref[...]

    @pl.when((pl.program_id(0) == 0) & (pl.program_id(1) == 0))
    def _():
        run_ref[...] = jnp.zeros_like(run_ref)

    lane = lax.broadcasted_iota(jnp.int32, lg.shape, 1)
    lanef = lane.astype(F32)
    earlier = (lax.broadcasted_iota(jnp.int32, (tm, tm), 0) > lax.broadcasted_iota(jnp.int32, (tm, tm), 1))
    earlier = jnp.where(earlier, 1.0, 0.0).astype(BF16)
    run = run_ref[0:1, :]
    route = jnp.zeros(lg.shape, F32)
    vals = []
    for r in range(TOP_K):
        mx = jnp.max(lg, axis=-1, keepdims=True)
        idx = jnp.min(jnp.where(lg == mx, lanef, float(ROUTE_W)), axis=-1, keepdims=True)
        hit = lanef == idx
        hitf = jnp.where(hit, 1.0, 0.0)
        before = jnp.dot(earlier, hitf.astype(BF16), preferred_element_type=F32)
        rank = jnp.sum(jnp.where(hit, run + before, 0.0), axis=-1, keepdims=True)
        run = run + jnp.sum(hitf, axis=0, keepdims=True)
        route = jnp.where(lane == r, idx, route)
        route = jnp.where(lane == TOP_K + r, rank, route)
        vals.append(mx)
        lg = jnp.where(hit, NEG_BIG, lg)
    ex = [jnp.exp(v - vals[0]) for v in vals]
    den = ex[0] + ex[1] + ex[2] + ex[3]
    for r in range(TOP_K):
        route = jnp.where(lane == 2 * TOP_K + r, ex[r] / den, route)
    route_ref[0] = route
    run_ref[...] = jnp.broadcast_to(run, run_ref.shape)
    cnt_ref[...] = jnp.broadcast_to(run, cnt_ref.shape)


def _merge(xy, oda, omla, odac, omlac, gates, mod, g2, wa, wb, wo, rw, rb, n_x_tiles, n_q_tiles):
    B, SA, D = xy.shape
    with_ctx = odac is not None
    const2 = lambda b, i: (0, 0)
    row_of = lambda b, i: jnp.where(i < n_x_tiles, b, B)
    row_spec = lambda w: pl.BlockSpec((1, ROW_TILE, w), lambda b, i: (b, i, 0))
    x_rows = lambda w: pl.BlockSpec((1, ROW_TILE, w), lambda b, i: (b, jnp.minimum(i, n_x_tiles - 1), 0))
    c_rows = lambda w: pl.BlockSpec((1, ROW_TILE, w), lambda b, i: (b, 0, 0))
    AW = oda.shape[2]
    att_specs = [x_rows(AW), x_rows(AW)] + ([c_rows(AW), c_rows(AW)] if with_ctx else [])
    att_args = [oda, omla] + ([odac, omlac] if with_ctx else [])
    SQ = n_q_tiles * ROW_TILE
    return pl.pallas_call(
        functools.partial(_merge_kernel, n_x_tiles=n_x_tiles, with_ctx=with_ctx),
        out_shape=(
            jax.ShapeDtypeStruct((B, SQ, D), F32),
            jax.ShapeDtypeStruct((B, SQ, D // 2), jnp.uint32),
            jax.ShapeDtypeStruct((B, SQ, ROUTE_W), F32),
            jax.ShapeDtypeStruct((8, ROUTE_W), F32),
        ),
        grid=(B, n_q_tiles),
        in_specs=[
            row_spec(D), *att_specs, row_spec(2 * D),
            pl.BlockSpec((1, 6, D), lambda b, i: (row_of(b, i), 0, 0)),
            pl.BlockSpec((1, D), const2),
            pl.BlockSpec(wa.shape, const2), pl.BlockSpec(wb.shape, const2), pl.BlockSpec(wo.shape, const2),
            pl.BlockSpec(rw.shape, const2), pl.BlockSpec(rb.shape, const2),
        ],
        out_specs=(row_spec(D), row_spec(D // 2), row_spec(ROUTE_W), pl.BlockSpec((8, ROUTE_W), const2)),
        scratch_shapes=[pltpu.VMEM((8, ROUTE_W), F32)],
        compiler_params=_cparams(("arbitrary", "arbitrary")),
        name="merge_post",
    )(xy, *att_args, gates, mod, g2, wa, wb, wo, rw, rb)


def _row_copy(src_ref, src_row, dst_ref, dst_row, sem):
    return pltpu.make_async_copy(src_ref.at[pl.ds(src_row, 1)], dst_ref.at[pl.ds(dst_row, 1)], sem)


def _dispatch_kernel(slot_ref, h_ref, buf_in_ref, buf_ref, sem):
    del buf_in_ref
    tm = h_ref.shape[0]

    for r in range(tm):
        for k in range(TOP_K):
            _row_copy(h_ref, r, buf_ref, slot_ref[0, 0, r * TOP_K + k], sem).start(priority=k % 2)
    for k in range(TOP_K):
        pltpu.make_async_copy(h_ref, buf_ref.at[pl.ds(0, tm)], sem).wait()


def _dispatch(slot3, h2, buf):
    T, D = h2.shape
    nt = T // MOVE_TILE
    return pl.pallas_call(
        _dispatch_kernel,
        out_shape=jax.ShapeDtypeStruct(buf.shape, buf.dtype),
        grid=(nt,),
        in_specs=[
            pl.BlockSpec((1, 1, MOVE_TILE * TOP_K), lambda t: (t, 0, 0), memory_space=pltpu.SMEM),
            pl.BlockSpec((MOVE_TILE, D), lambda t: (t, 0)),
            pl.BlockSpec(memory_space=pl.ANY),
        ],
        out_specs=pl.BlockSpec(memory_space=pl.ANY),
        scratch_shapes=[pltpu.SemaphoreType.DMA(())],
        input_output_aliases={2: 0},
        compiler_params=_cparams(("arbitrary",)),
        name="moe_dispatch",
    )(slot3, h2, buf)


def _expert_kernel(be_ref, nu_ref, x_ref, wgu_ref, bgu_ref, wdn_ref, bdn_ref, o_ref):
    n = pl.program_id(0)

    @pl.when(n < nu_ref[0])
    def _():
        xa, xb = _unpack_halves(x_ref[...])
        half = xa.shape[1]
        h = (jnp.dot(xa.astype(BF16), wgu_ref[0, 0:half, :], preferred_element_type=F32)
             + jnp.dot(xb.astype(BF16), wgu_ref[0, half:2 * half, :], preferred_element_type=F32) + bgu_ref[0])
        gate = jnp.minimum(h[:, :D_FF], SWIGLU_LIMIT)
        lin = jnp.clip(h[:, D_FF:], -SWIGLU_LIMIT, SWIGLU_LIMIT)
        act = (lin + 1.0) * gate * jax.nn.sigmoid(SWIGLU_ALPHA * gate)
        o_ref[...] = _pack_halves(jnp.dot(act.astype(BF16), wdn_ref[0], preferred_element_type=F32) + bdn_ref[0])

    @pl.when(n >= nu_ref[0])
    def _():
        o_ref[...] = jnp.zeros_like(o_ref)


def _experts(block_e, n_used, buf, wgu, bgu, wdn, bdn):
    R, DP = buf.shape
    D = 2 * DP
    NB = R // EXPERT_BLOCK
    E = wgu.shape[0]
    return pl.pallas_call(
        _expert_kernel,
        out_shape=jax.ShapeDtypeStruct((R, DP), jnp.uint32),
        grid_spec=pltpu.PrefetchScalarGridSpec(
            num_scalar_prefetch=2,
            grid=(NB,),
            in_specs=[
                pl.BlockSpec((EXPERT_BLOCK, DP), lambda n, be, nu: (jnp.minimum(n, nu[0] - 1), 0)),
                pl.BlockSpec((1, D, 2 * D_FF), lambda n, be, nu: (be[n], 0, 0)),
                pl.BlockSpec((1, 1, 2 * D_FF), lambda n, be, nu: (be[n], 0, 0)),
                pl.BlockSpec((1, D_FF, D), lambda n, be, nu: (be[n], 0, 0)),
                pl.BlockSpec((1, 1, D), lambda n, be, nu: (be[n], 0, 0)),
            ],
            out_specs=pl.BlockSpec((EXPERT_BLOCK, DP), lambda n, be, nu: (n, 0)),
        ),
        compiler_params=_cparams(("arbitrary",)),
        name="expert_mlp",
    )(block_e, n_used, buf, wgu, bgu.reshape(E, 1, 2 * D_FF), wdn, bdn.reshape(E, 1, D))


def _combine_kernel(slot_ref, nslot_ref, x_ref, route_ref, moda_ref, modb_ref, fg_ref, y_hbm, o_ref, ybuf, sems, *, final):
    t = pl.program_id(0)
    nt = pl.num_programs(0)
    tm = x_ref.shape[0]
    cur = lax.rem(t, 2)

    def gather(s_ref, half):
        for r in range(tm):
            for k in range(TOP_K):
                _row_copy(y_hbm, s_ref[0, 0, r * TOP_K + k], ybuf.at[half, k], r,
                          sems.at[half]).start(priority=k % 2)

    @pl.when(t == 0)
    def _():
        gather(slot_ref, 0)

    @pl.when(t + 1 < nt)
    def _():
        gather(nslot_ref, 1 - cur)

    for k in range(TOP_K):
        pltpu.make_async_copy(y_hbm.at[pl.ds(0, tm)], ybuf.at[cur, k], sems.at[cur]).wait()

    w = route_ref[:, 2 * TOP_K:3 * TOP_K]
    f_lo = f_hi = None
    for k in range(TOP_K):
        y_lo, y_hi = _unpack_halves(ybuf[cur, k])
        f_lo = w[:, k:k + 1] * y_lo if k == 0 else f_lo + w[:, k:k + 1] * y_lo
        f_hi = w[:, k:k + 1] * y_hi if k == 0 else f_hi + w[:, k:k + 1] * y_hi
    f = jnp.concatenate([f_lo, f_hi], axis=-1)
    hm = tm // 2
    for (lo, hi, m_ref) in ((0, hm, moda_ref), (hm, tm, modb_ref)):
        xn = x_ref[lo:hi, :] + m_ref[0, 5:6, :] * f[lo:hi]
        if final:
            xn = _rms_rows(xn) * fg_ref[...]
        o_ref[lo:hi, :] = xn


def _combine(x2, y_sorted, slot3, route, mod, fg, n_x_tiles, n_q_tiles, n_batch, final):
    T, D = x2.shape
    nt = T // MOVE_TILE
    row_of = lambda u: jnp.where(u % n_q_tiles < n_x_tiles, u // n_q_tiles, n_batch)
    slot_block = (1, 1, MOVE_TILE * TOP_K)
    return pl.pallas_call(
        functools.partial(_combine_kernel, final=final),
        out_shape=jax.ShapeDtypeStruct((T, D), F32),
        grid=(nt,),
        in_specs=[
            pl.BlockSpec(slot_block, lambda t: (t, 0, 0), memory_space=pltpu.SMEM),
            pl.BlockSpec(slot_block, lambda t: (jnp.minimum(t + 1, nt - 1), 0, 0), memory_space=pltpu.SMEM),
            pl.BlockSpec((MOVE_TILE, D), lambda t: (t, 0)),
            pl.BlockSpec((MOVE_TILE, ROUTE_W), lambda t: (t, 0)),
            pl.BlockSpec((1, 6, D), lambda t: (row_of(2 * t), 0, 0)),
            pl.BlockSpec((1, 6, D), lambda t: (row_of(2 * t + 1), 0, 0)),
            pl.BlockSpec((1, D), lambda t: (0, 0)),
            pl.BlockSpec(memory_space=pl.ANY),
        ],
        out_specs=pl.BlockSpec((MOVE_TILE, D), lambda t: (t, 0)),
        scratch_shapes=[pltpu.VMEM((2, TOP_K, MOVE_TILE, D // 2), jnp.uint32), pltpu.SemaphoreType.DMA((2,))],
        compiler_params=_cparams(("arbitrary",)),
        name="moe_combine",
    )(slot3, slot3, x2, route, mod, mod, fg, y_sorted)


def _block_layout(counts, n_blocks):
    sizes = counts[0, :N_EXPERTS].astype(jnp.int32)
    padded = ((sizes + EXPERT_BLOCK - 1) // EXPERT_BLOCK) * EXPERT_BLOCK
    pend = jnp.cumsum(padded)
    pstart = pend - padded
    first_row = jnp.arange(n_blocks, dtype=jnp.int32) * EXPERT_BLOCK
    block_e = jnp.minimum(jnp.sum(pend[None, :] <= first_row[:, None], axis=1), N_EXPERTS - 1).astype(jnp.int32)
    n_used = (pend[-1] // EXPERT_BLOCK).astype(jnp.int32).reshape(1)
    return pstart, block_e, n_used


def _rope_tables(S, n_ctx):
    rows = jnp.repeat(jnp.arange(S // GRID_W, dtype=F32), GRID_W)
    cols = jnp.tile(jnp.arange(GRID_W, dtype=F32), S // GRID_W)
    half = DA_HD // 2
    freqs = ROPE_BASE ** (-jnp.arange(0, half, 2, dtype=F32) / half)
    ar = rows[:, None] * freqs
    ac = cols[:, None] * freqs
    ang = jnp.concatenate([ar, ar, ac, ac], axis=-1)
    cos = jnp.concatenate([jnp.cos(ang), jnp.ones((n_ctx, DA_HD), F32)], axis=0)
    sin = jnp.concatenate([jnp.sin(ang), jnp.zeros((n_ctx, DA_HD), F32)], axis=0)
    cos = jnp.tile(cos, (1, HEAD_W // DA_HD))
    sin = jnp.tile(sin, (1, HEAD_W // DA_HD))
    lo = (jnp.arange(HEAD_W) % (DA_HD // 2)) < (DA_HD // 4)
    return cos, jnp.where(lo, -sin, 0.0), jnp.where(lo, 0.0, sin)


def kernel(x, c, ctx, c_ctx, ada_w, ada_b, norm1_g, norm2_g, w_in, da_lq1, da_lk1, da_lq2, da_lk2, da_subln_g, mla_qa_g, mla_kva_g, mla_wqb, mla_wkvb, w_br_a, w_br_b, w_out, router_w, router_b, moe_wgu, moe_bgu, moe_wdn, moe_bdn, final_g):
    B, S, D = x.shape
    CT = ctx.shape[1]
    L = ada_w.shape[0]
    SA = S + CT
    assert S % ATT_TILE == 0 and CT == ROW_TILE and S % GRID_W == 0 and MOVE_TILE == 2 * ROW_TILE
    assert (B * SA) % MOVE_TILE == 0 and (B * S) % MOVE_TILE == 0
    n_x_tiles = S // ROW_TILE
    n_tiles = SA // ROW_TILE
    H = DA_HEADS

    xy = jnp.concatenate([x, ctx], axis=1)
    cc = jnp.concatenate([c, c_ctx[None, :], jnp.zeros((16 - B - 1, D), F32)], axis=0)
    mod_all = _ada_all(cc, ada_w, ada_b).reshape(L, 16, 6, D)
    cos, sa, sb = _rope_tables(S, CT)

    qw = H * HEAD_W
    o_cq = 3 * qw
    o_kr = o_cq + MLA_Q_RANK + MLA_KV_RANK
    o_g = o_kr + MLA_ROPE
    wm = w_in[:, :, :o_kr].astype(BF16)
    wkr = jnp.pad(w_in[:, :, o_kr:o_g], ((0, 0), (0, 0), (0, HEAD_W - MLA_ROPE))).astype(BF16)
    wg = w_in[:, :, o_g:].astype(BF16)
    wqb4 = mla_wqb.reshape(L, MLA_Q_RANK, MLA_HEADS, MLA_NOPE + MLA_ROPE)
    wqb_n = wqb4[..., :MLA_NOPE].reshape(L, MLA_Q_RANK, MLA_HEADS * MLA_NOPE)
    wqb_r = jnp.pad(wqb4[..., MLA_NOPE:], ((0, 0), (0, 0), (0, 0), (0, HEAD_W - MLA_ROPE)))
    wqb = jnp.concatenate([wqb_n, wqb_r.reshape(L, MLA_Q_RANK, MLA_HEADS * HEAD_W)], axis=-1).astype(BF16)
    wkvb4 = mla_wkvb.reshape(L, MLA_KV_RANK, MLA_HEADS, MLA_NOPE + MLA_V)
    wkvb = jnp.concatenate([wkvb4[..., :MLA_NOPE].reshape(L, MLA_KV_RANK, -1),
                            wkvb4[..., MLA_NOPE:].reshape(L, MLA_KV_RANK, -1)], axis=-1).astype(BF16)
    wa = w_br_a.astype(BF16)
    wb = w_br_b.astype(BF16)
    wo = w_out.astype(BF16)
    RW = ROUTE_W
    rw = jnp.pad(router_w, ((0, 0), (0, 0), (0, RW - N_EXPERTS))).astype(BF16)
    rb = jnp.pad(router_b, ((0, 0), (0, RW - N_EXPERTS)), constant_values=NEG_BIG)
    wgu = moe_wgu.astype(BF16)
    wdn = moe_wdn.astype(BF16)
    NB = (B * SA * TOP_K + EXPERT_BLOCK - 1) // EXPERT_BLOCK + N_EXPERTS
    buf = jnp.zeros((NB * EXPERT_BLOCK, D // 2), jnp.uint32)

    for l in range(L):
        last = l == L - 1
        n_q = n_x_tiles if last else n_tiles
        mod = mod_all[l]
        lam_init = 0.8 - 0.6 * math.exp(-0.3 * l)
        lam = (jnp.exp(jnp.sum(da_lq1[l] * da_lk1[l])) - jnp.exp(jnp.sum(da_lq2[l] * da_lk2[l])) + lam_init)
        scal = jnp.stack([lam, jnp.asarray(1.0 - lam_init, F32)]).astype(F32)

        daq, dak, dav, mq, mk, mv, gates = _pre(
            xy, mod, norm1_g[l][None], cos, sa, sb, wm[l], wkr[l], wg[l],
            mla_qa_g[l][None], mla_kva_g[l][None], wqb[l], wkvb[l], n_x_tiles)
        oda, odac, omla, omlac = _attention(daq, dak, dav, mq, mk, mv, S, not last, scal, da_subln_g[l][None])
        xy2, h2, route, counts = _merge(xy, oda, omla, odac, omlac, gates, mod, norm2_g[l][None], wa[l], wb[l], wo[l],
                                 rw[l], rb[l][None], n_x_tiles, n_q)

        SQ = n_q * ROW_TILE
        T = B * SQ
        pstart, block_e, n_used = _block_layout(counts, NB)
        route2 = route.reshape(T, ROUTE_W)
        e_idx = route2[:, 0:TOP_K].astype(jnp.int32)
        rank = route2[:, TOP_K:2 * TOP_K].astype(jnp.int32)
        base = jnp.sum(jnp.where(e_idx[:, :, None] == jnp.arange(N_EXPERTS, dtype=jnp.int32), pstart, 0), axis=-1)
        slot3 = (base + rank).reshape(T // MOVE_TILE, 1, MOVE_TILE * TOP_K)
        buf = _dispatch(slot3, h2.reshape(T, D // 2), buf)
        out_sorted = _experts(block_e, n_used, buf, wgu[l], moe_bgu[l], wdn[l], moe_bdn[l])
        xy = _combine(xy2.reshape(T, D), out_sorted, slot3, route2, mod, final_g[None],
                      n_x_tiles, n_q, B, last).reshape(B, SQ, D)
    return xy
```
